```python
import math
import jax, jax.numpy as jnp
from jax import lax
import numpy as np

D_MODEL = 1024
BATCH = 16
SEQ = 2048
DEPTH = 4

SSD_HEAD_DIM = 64
SSD_WIDTH = D_MODEL
SSD_HEADS = SSD_WIDTH // SSD_HEAD_DIM
SSD_GROUPS = 4
SSD_STATE = 128
SSD_CONV = 4
SSD_CHUNK = 128
SSD_CONV_CH = SSD_WIDTH + 2 * SSD_GROUPS * SSD_STATE

DA_QK_DIM = 64
DA_V_DIM = 128
DA_HEADS = D_MODEL // DA_V_DIM
DA_QK_WIDTH = DA_HEADS * 2 * DA_QK_DIM
DA_WIDTH = DA_HEADS * DA_V_DIM
Q_BLOCK = 128
REL_BUCKETS = 32
REL_MAX_DIST = 128

EVEN_SPLITS = (SSD_WIDTH, SSD_CONV_CH, SSD_HEADS, DA_QK_WIDTH, DA_QK_WIDTH, DA_WIDTH, DA_WIDTH)
EVEN_IN = sum(EVEN_SPLITS)
EVEN_MIX = SSD_WIDTH + DA_WIDTH

HG_HEAD_DIM = 128
HG_WIDTH = 2 * D_MODEL
HG_HEADS = HG_WIDTH // HG_HEAD_DIM
HG_CHUNK = 16
ODD_IN = 4 * HG_WIDTH

N_EVEN = (DEPTH + 1) // 2
N_ODD = DEPTH // 2
EPS = 1e-6

kernel_name = "ssd_diffattn_hgrn2_hybrid"


def rmsnorm(x, w):
    xf = x.astype(jnp.float32)
    y = xf * lax.rsqrt(jnp.mean(xf * xf, axis=-1, keepdims=True) + EPS)
    return (y * w.astype(jnp.float32)).astype(x.dtype)


def causal_dwconv(x, w, b):
    K, C = w.shape
    y = lax.conv_general_dilated(x, w[:, None, :].astype(x.dtype), window_strides=(1,),
                                 padding=[(K - 1, 0)], dimension_numbers=("NWC", "WIO", "NWC"),
                                 feature_group_count=C)
    return y + b.astype(x.dtype)


def segsum(a):
    T = a.shape[-1]
    cs = jnp.cumsum(a, axis=-1)
    diff = cs[..., :, None] - cs[..., None, :]
    return jnp.where(jnp.tril(jnp.ones((T, T), bool)), diff, -jnp.inf)


def ssd_scan(x, dt, A, Bm, Cm):
    b, L, H, P = x.shape
    G, N = Bm.shape[2], Bm.shape[3]
    R = H // G
    c, l = L // SSD_CHUNK, SSD_CHUNK
    X = (x.astype(jnp.float32) * dt[..., None]).reshape(b, c, l, G, R, P)
    a = (dt * A).reshape(b, c, l, G, R).transpose(0, 1, 3, 4, 2)
    Bc = Bm.reshape(b, c, l, G, N)
    Cc = Cm.reshape(b, c, l, G, N)
    a_cs = jnp.cumsum(a, axis=-1)
    CB = jnp.einsum("bclgn,bcsgn->bcgls", Cc, Bc)
    scores = CB[:, :, :, None] * jnp.exp(segsum(a))
    y_diag = jnp.einsum("bcgrls,bcsgrp->bclgrp", scores, X)
    decay_states = jnp.exp(a_cs[..., -1:] - a_cs)
    states = jnp.einsum("bclgn,bcgrl,bclgrp->bcgrpn", Bc, decay_states, X)
    chunk_a = jnp.pad(a_cs[..., -1].transpose(0, 2, 3, 1), ((0, 0), (0, 0), (0, 0), (1, 0)))
    decay_chunk = jnp.exp(segsum(chunk_a))
    states = jnp.concatenate([jnp.zeros_like(states[:, :1]), states], axis=1)
    new_states = jnp.einsum("bgrzc,bcgrpn->bzgrpn", decay_chunk, states)
    prev = new_states[:, :-1]
    y_off = jnp.einsum("bclgn,bcgrpn,bcgrl->bclgrp", Cc, prev, jnp.exp(a_cs))
    return (y_diag + y_off).reshape(b, L, H, P)


def t5_bucket(rel):
    n = jnp.maximum(rel, 0)
    max_exact = REL_BUCKETS // 2
    large = max_exact + (jnp.log(jnp.maximum(n, 1).astype(jnp.float32) / max_exact)
                         / math.log(REL_MAX_DIST / max_exact) * (REL_BUCKETS - max_exact)).astype(jnp.int32)
    large = jnp.minimum(large, REL_BUCKETS - 1)
    return jnp.where(n < max_exact, n, large)


def diff_attention(q, k, v, lam, rel_table):
    b, L, H, _, d = q.shape
    nb = L // Q_BLOCK
    scale = d ** -0.5
    kpos = jnp.arange(L)
    qb = q.reshape(b, nb, Q_BLOCK, H, 2, d).transpose(1, 0, 2, 3, 4, 5)

    def block(args):
        qi, i = args
        qpos = i * Q_BLOCK + jnp.arange(Q_BLOCK)
        rel = qpos[:, None] - kpos[None, :]
        bias = rel_table[t5_bucket(rel)].transpose(2, 0, 1).astype(jnp.float32)
        s = jnp.einsum("bqhcd,bkhcd->bhcqk", qi, k).astype(jnp.float32) * scale + bias[None, :, None]
        s = jnp.where(rel >= 0, s, -jnp.inf)
        p = jax.nn.softmax(s, axis=-1)
        w = p[:, :, 0] - lam * p[:, :, 1]
        return jnp.einsum("bhqk,bkhv->bqhv", w.astype(v.dtype), v)

    out = lax.map(block, (qb, jnp.arange(nb)))
    return out.transpose(1, 0, 2, 3, 4).reshape(b, L, H, v.shape[-1])


def ssd_diffattn_mixer(u, w_in, w_out, conv_w, conv_b, dt_bias, A_log, D_skip, ssd_norm_w,
                       lq1, lk1, lq2, lk2, subln_w, rel_table, layer_idx):
    b, L, _ = u.shape
    idx = [int(s) for s in np.cumsum(EVEN_SPLITS)[:-1]]
    z, xBC, dt, q, k, v, g = jnp.split(u @ w_in, idx, axis=-1)
    xBC = jax.nn.silu(causal_dwconv(xBC, conv_w, conv_b))
    xs, Bm, Cm = jnp.split(xBC, [SSD_WIDTH, SSD_WIDTH + SSD_GROUPS * SSD_STATE], axis=-1)
    xs = xs.reshape(b, L, SSD_HEADS, SSD_HEAD_DIM)
    Bm = Bm.reshape(b, L, SSD_GROUPS, SSD_STATE)
    Cm = Cm.reshape(b, L, SSD_GROUPS, SSD_STATE)
    dt = jax.nn.softplus(dt.astype(jnp.float32) + dt_bias.astype(jnp.float32))
    A = -jnp.exp(A_log.astype(jnp.float32))
    y = ssd_scan(xs, dt, A, Bm, Cm) + xs * D_skip[:, None]
    y = y.astype(u.dtype).reshape(b, L, SSD_WIDTH)
    y_a = rmsnorm(y * jax.nn.silu(z), ssd_norm_w)
    q = q.reshape(b, L, DA_HEADS, 2, DA_QK_DIM)
    k = k.reshape(b, L, DA_HEADS, 2, DA_QK_DIM)
    v = v.reshape(b, L, DA_HEADS, DA_V_DIM)
    lam_init = 0.8 - 0.6 * math.exp(-0.3 * layer_idx)
    lam = (jnp.exp(jnp.sum(lq1.astype(jnp.float32) * lk1.astype(jnp.float32)))
           - jnp.exp(jnp.sum(lq2.astype(jnp.float32) * lk2.astype(jnp.float32))) + lam_init)
    o = diff_attention(q, k, v, lam, rel_table)
    o = rmsnorm(o, subln_w) * (1.0 - lam_init)
    y_b = o.reshape(b, L, DA_WIDTH) * jax.nn.silu(g)
    return jnp.concatenate([y_a, y_b], axis=-1) @ w_out


def gated_linear_recurrence(q, k, v, log_f):
    b, L, H, dk = q.shape
    dv = v.shape[-1]
    C = HG_CHUNK
    nc = L // C

    def to_chunks(t):
        return t.reshape(b, nc, C, H, t.shape[-1]).transpose(1, 0, 3, 2, 4)

    causal = jnp.tril(jnp.ones((C, C), bool))

    def step(S, inp):
        qi, ki, vi, gi = inp
        bcum = jnp.cumsum(gi.astype(jnp.float32), axis=2)
        diff = bcum[:, :, :, None, :] - bcum[:, :, None, :, :]
        decay = jnp.exp(jnp.where(causal[:, :, None], diff, -jnp.inf))
        att = jnp.sum(qi[:, :, :, None, :] * ki[:, :, None, :, :] * decay, axis=-1)
        o = (jnp.einsum("bhts,bhsv->bhtv", att, vi)
             + jnp.einsum("bhtd,bhdv->bhtv", qi * jnp.exp(bcum), S))
        blast = bcum[:, :, -1:, :]
        S = (jnp.exp(blast[:, :, 0, :, None]) * S
             + jnp.einsum("bhsd,bhsv->bhdv", ki * jnp.exp(blast - bcum), vi))
        return S, o

    S0 = jnp.zeros((b, H, dk, dv), jnp.float32)
    _, o = lax.scan(step, S0, (to_chunks(q), to_chunks(k), to_chunks(v), to_chunks(log_f)))
    return o.transpose(1, 0, 3, 2, 4).reshape(b, L, H, dv).astype(v.dtype)


def hgrn2_mixer(u, w_in, w_out, lower_bounds, norm_w, layer_idx):
    b, L, _ = u.shape
    q, f, i, g = jnp.split(u @ w_in, 4, axis=-1)
    lb_all = jax.nn.softmax(lower_bounds.astype(jnp.float32), axis=0)
    lb_all = jnp.cumsum(lb_all, axis=0) - lb_all[0]
    lb = lb_all[layer_idx]
    f = lb + (1.0 - lb) * jax.nn.sigmoid(f.astype(jnp.float32))
    log_f = jnp.log(f)
    k = (1.0 - f).astype(u.dtype)
    q = jax.nn.silu(q)
    hs = lambda t: t.reshape(b, L, HG_HEADS, HG_HEAD_DIM)
    o = gated_linear_recurrence(hs(q), hs(k), hs(i), hs(log_f))
    o = rmsnorm(o, norm_w).reshape(b, L, HG_WIDTH) * jax.nn.silu(g)
    return o @ w_out


def setup_inputs(seed: int = 0) -> dict:
    key = jax.random.key(seed)
    ks = jax.random.split(key, 24)
    nrm = lambda k, shape, s: jax.random.normal(k, shape, jnp.float32) * s
    dt = jnp.exp(jax.random.uniform(ks[7], (N_EVEN, SSD_HEADS), jnp.float32)
                 * (math.log(0.1) - math.log(1e-3)) + math.log(1e-3))
    dt = jnp.maximum(dt, 1e-4)
    return {
        "x": jax.random.normal(ks[0], (BATCH, SEQ, D_MODEL), jnp.float32),
        "norm_w": 1.0 + nrm(ks[1], (DEPTH, D_MODEL), 0.02),
        "final_norm_w": 1.0 + nrm(ks[2], (D_MODEL,), 0.02),
        "rel_bias": nrm(ks[3], (REL_BUCKETS, DA_HEADS), 0.1),
        "even_w_in": nrm(ks[4], (N_EVEN, D_MODEL, EVEN_IN), D_MODEL ** -0.5),
        "even_w_out": nrm(ks[5], (N_EVEN, EVEN_MIX, D_MODEL), EVEN_MIX ** -0.5),
        "conv_w": nrm(ks[6], (N_EVEN, SSD_CONV, SSD_CONV_CH), SSD_CONV ** -0.5),
        "conv_b": nrm(ks[8], (N_EVEN, SSD_CONV_CH), 0.02),
        "dt_bias": dt + jnp.log(-jnp.expm1(-dt)),
        "A_log": jnp.log(jax.random.uniform(ks[9], (N_EVEN, SSD_HEADS), jnp.float32, 1.0, 16.0)),
        "D_skip": 1.0 + nrm(ks[10], (N_EVEN, SSD_HEADS), 0.02),
        "ssd_norm_w": 1.0 + nrm(ks[11], (N_EVEN, SSD_WIDTH), 0.02),
        "lambda_q1": nrm(ks[12], (N_EVEN, DA_QK_DIM), 0.1),
        "lambda_k1": nrm(ks[13], (N_EVEN, DA_QK_DIM), 0.1),
        "lambda_q2": nrm(ks[14], (N_EVEN, DA_QK_DIM), 0.1),
        "lambda_k2": nrm(ks[15], (N_EVEN, DA_QK_DIM), 0.1),
        "subln_w": 1.0 + nrm(ks[16], (N_EVEN, DA_V_DIM), 0.02),
        "odd_w_in": nrm(ks[17], (N_ODD, D_MODEL, ODD_IN), D_MODEL ** -0.5),
        "odd_w_out": nrm(ks[18], (N_ODD, HG_WIDTH, D_MODEL), HG_WIDTH ** -0.5),
        "hgrn_lower_bounds": nrm(ks[19], (DEPTH, HG_WIDTH), 0.1),
        "hgrn_norm_w": 1.0 + nrm(ks[20], (N_ODD, HG_HEAD_DIM), 0.02),
    }


def reference(x, norm_w, final_norm_w, rel_bias, even_w_in, even_w_out, conv_w, conv_b, dt_bias,
              A_log, D_skip, ssd_norm_w, lambda_q1, lambda_k1, lambda_q2, lambda_k2, subln_w,
              odd_w_in, odd_w_out, hgrn_lower_bounds, hgrn_norm_w):
    h = x
    for layer in range(DEPTH):
        u = rmsnorm(h, norm_w[layer])
        if layer % 2 == 0:
            e = layer // 2
            h = h + ssd_diffattn_mixer(u, even_w_in[e], even_w_out[e], conv_w[e], conv_b[e],
                                       dt_bias[e], A_log[e], D_skip[e], ssd_norm_w[e],
                                       lambda_q1[e], lambda_k1[e], lambda_q2[e], lambda_k2[e],
                                       subln_w[e], rel_bias, layer)
        else:
            o = layer // 2
            h = h + hgrn2_mixer(u, odd_w_in[o], odd_w_out[o], hgrn_lower_bounds,
                                hgrn_norm_w[o], layer)
    return rmsnorm(h, final_norm_w)
```

```python
import functools
import math

import jax
import jax.numpy as jnp
from jax import lax
from jax.experimental import pallas as pl
from jax.experimental.pallas import tpu as pltpu

F32 = jnp.float32
BF16 = jnp.bfloat16
EPS = 1e-6
NEG_INF = float("-inf")

LANES = 128
SSD_HEAD_DIM = 64
SSD_GROUPS = 4
SSD_STATE = 128
SSD_CONV = 4
SSD_CHUNK = 128
DA_QK_DIM = 64
DA_V_DIM = 128
REL_BUCKETS = 32
REL_MAX_DIST = 128
HG_HEAD_DIM = 128
HG_CHUNK = 128
HG_DIRECT = 8
HG_HEADS_PER_STEP = 4
ATTN_BLOCK = 256
PROJ_ROWS = 512
VMEM_LIMIT = 56 * 1024 * 1024


def _dot(a, b):
    return jnp.dot(a, b, preferred_element_type=F32)


def _dot_nt(a, b):
    return lax.dot_general(a, b, (((1,), (1,)), ((), ())), preferred_element_type=F32)


def _dot_tn(a, b):
    return lax.dot_general(a, b, (((0,), (0,)), ((), ())), preferred_element_type=F32)


def _split_bf16(x, parts):
    out = []
    r = x
    for _ in range(parts):
        p = r.astype(BF16)
        out.append(p)
        r = r - p.astype(F32)
    return out


def _dot_exact_rhs(x, m_bf16, parts):
    return sum(_dot(p, m_bf16) for p in _split_bf16(x, parts))


def _dot_exact_lhs(m_bf16, x, parts):
    return sum(_dot(m_bf16, p) for p in _split_bf16(x, parts))


def _sigmoid(x):
    return 1.0 / (1.0 + jnp.exp(-x))


def _silu(x):
    return x * _sigmoid(x)


def _rms_scale(x):
    return lax.rsqrt(jnp.mean(x * x, axis=-1, keepdims=True) + EPS)


def _const_spec(shape):
    nd = len(shape)
    return pl.BlockSpec(shape, lambda *_: (0,) * nd, pipeline_mode=pl.Buffered(1))


def _params(sem):
    return pltpu.CompilerParams(dimension_semantics=sem, vmem_limit_bytes=VMEM_LIMIT)


def _proj_even_body(h_ref, nw_ref, w_ref, z_ref, xbc_ref, q_ref, k_ref, v_ref, g_ref, dt_ref, *, d):
    x = h_ref[...]
    u = (x * _rms_scale(x) * nw_ref[...]).astype(BF16)
    col = 0
    for ref in (z_ref, xbc_ref, q_ref, k_ref, v_ref, g_ref, dt_ref):
        n = ref.shape[1]
        step = min(n, 512)
        for j in range(0, n, step):
            ref[:, j:j + step] = _dot(u, w_ref[:, col + j:col + j + step]).astype(ref.dtype)
        col += n


def _proj_even(h, nw, w, tm):
    t, d = h.shape
    widths = (d, 2 * d, d, d, d, d, LANES)
    dtypes = (F32, F32, BF16, BF16, BF16, F32, F32)
    row = lambda n: pl.BlockSpec((tm, n), lambda i: (i, 0))
    return pl.pallas_call(
        functools.partial(_proj_even_body, d=d),
        grid=(t // tm,),
        in_specs=[row(d), _const_spec((1, d)), _const_spec(w.shape)],
        out_specs=[row(n) for n in widths],
        out_shape=[jax.ShapeDtypeStruct((t, n), dt) for n, dt in zip(widths, dtypes)],
        compiler_params=_params(("parallel",)),
        name="proj_even",
    )(h, nw, w)


def _proj_odd_body(h_ref, nw_ref, w_ref, lb_ref, q_ref, k_ref, lf_ref, v_ref, g_ref):
    x = h_ref[...]
    u = (x * _rms_scale(x) * nw_ref[...]).astype(BF16)
    n = q_ref.shape[1]
    step = 512
    for j in range(0, n, step):
        cs = slice(j, j + step)
        q_ref[:, cs] = _silu(_dot(u, w_ref[:, j:j + step]))
        lb = lb_ref[:, cs]
        f = lb + (1.0 - lb) * _sigmoid(_dot(u, w_ref[:, n + j:n + j + step]))
        lf_ref[:, cs] = jnp.log(f)
        k_ref[:, cs] = 1.0 - f
        v_ref[:, cs] = _dot(u, w_ref[:, 2 * n + j:2 * n + j + step]).astype(BF16)
        g_ref[:, cs] = _dot(u, w_ref[:, 3 * n + j:3 * n + j + step])


def _proj_odd(h, nw, w, lb, tm):
    t, d = h.shape
    n = w.shape[1] // 4
    dtypes = (F32, F32, F32, BF16, F32)
    row = lambda m: pl.BlockSpec((tm, m), lambda i: (i, 0))
    return pl.pallas_call(
        _proj_odd_body,
        grid=(t // tm,),
        in_specs=[row(d), _const_spec((1, d)), _const_spec(w.shape), _const_spec((1, n))],
        out_specs=[row(n) for _ in dtypes],
        out_shape=[jax.ShapeDtypeStruct((t, n), dt) for dt in dtypes],
        compiler_params=_params(("parallel",)),
        name="proj_odd",
    )(h, nw, w, lb)


def _out_proj_body(*refs, n_act, final):
    h_ref = refs[0]
    acts = refs[1:1 + n_act]
    ws = refs[1 + n_act:1 + 2 * n_act]
    rest = refs[1 + 2 * n_act:]
    acc = h_ref[...]
    for a_ref, w_ref in zip(acts, ws):
        acc = acc + _dot(a_ref[...], w_ref[...])
    if final:
        fw_ref, o_ref = rest
        acc = acc * _rms_scale(acc) * fw_ref[...]
    else:
        (o_ref,) = rest
    o_ref[...] = acc


def _out_proj(h, acts, ws, tm, final_w=None):
    t, d = h.shape
    final = final_w is not None
    row = lambda n: pl.BlockSpec((tm, n), lambda i: (i, 0))
    in_specs = [row(d)] + [row(a.shape[1]) for a in acts] + [_const_spec(w.shape) for w in ws]
    args = [h, *acts, *ws]
    if final:
        in_specs.append(_const_spec((1, d)))
        args.append(final_w)
    return pl.pallas_call(
        functools.partial(_out_proj_body, n_act=len(acts), final=final),
        grid=(t // tm,),
        in_specs=in_specs,
        out_specs=row(d),
        out_shape=jax.ShapeDtypeStruct((t, d), F32),
        compiler_params=_params(("parallel",)),
        name="out_proj",
    )(*args)


def _softplus(x):
    return jnp.maximum(x, 0.0) + jnp.log(1.0 + jnp.exp(-jnp.abs(x)))


def _ssd_body(xbc_ref, dt_ref, z_ref, cw_ref, cb_ref, dtb_ref, a_ref, dsk_ref, nw_ref, tri_ref,
              exp_ref, y_ref, ext_ref, st_ref, yacc_ref, *, width):
    cs = SSD_CHUNK
    gw = width // SSD_GROUPS
    hpg = gw // SSD_HEAD_DIM
    pad = 8
    c = pl.program_id(1)

    @pl.when(c == 0)
    def _():
        ext_ref[0:pad, :] = jnp.zeros((pad, ext_ref.shape[1]), F32)
        st_ref[...] = jnp.zeros(st_ref.shape, F32)

    xin = xbc_ref[...]
    ext_ref[pad:pad + cs, :] = xin
    conv = cb_ref[...] + cw_ref[SSD_CONV - 1:SSD_CONV, :] * xin
    for j in range(1, SSD_CONV):
        conv = conv + cw_ref[SSD_CONV - 1 - j:SSD_CONV - j, :] * ext_ref[pad - j:pad - j + cs, :]
    ext_ref[0:pad, :] = xin[cs - pad:cs, :]
    act = _silu(conv)

    dt = _softplus(dt_ref[...] + dtb_ref[...])
    a = dt * a_ref[...]
    a_cs = _dot_exact_lhs(tri_ref[...], a, 3)
    a_cs_t = a_cs.T
    dt_t = dt.T
    a_last = a_cs[cs - 1:cs, :]
    per_head = jnp.concatenate(
        [dt * jnp.exp(a_last - a_cs), jnp.exp(a_cs), jnp.broadcast_to(jnp.exp(a_last), (8, LANES))], axis=0)
    per_col = _dot_exact_rhs(per_head, exp_ref[...], 2)
    w_state = per_col[0:cs]
    e_acs = per_col[cs:2 * cs]
    e_last = per_col[2 * cs:2 * cs + 1]

    row = lax.broadcasted_iota(jnp.int32, (cs, cs), 0)
    colm = lax.broadcasted_iota(jnp.int32, (cs, cs), 1)
    tril = row >= colm
    lane = lax.broadcasted_iota(jnp.int32, (cs, LANES), 1)
    low_half = lane < SSD_HEAD_DIM

    for g in range(SSD_GROUPS):
        gs = slice(g * gw, (g + 1) * gw)
        xg = act[:, gs]
        bg = act[:, width + g * SSD_STATE:width + (g + 1) * SSD_STATE].astype(BF16)
        cg = act[:, width + (SSD_GROUPS + g) * SSD_STATE:width + (SSD_GROUPS + g + 1) * SSD_STATE].astype(BF16)
        cb = _dot_nt(cg, bg)
        s_prev = st_ref[g]
        y_off = _dot(cg, s_prev.astype(BF16)) * e_acs[:, gs]
        xd = (xg * w_state[:, gs]).astype(BF16)
        st_ref[g] = s_prev * e_last[:, gs] + _dot_tn(bg, xd)
        for p in range(gw // LANES):
            xp = xg[:, p * LANES:(p + 1) * LANES]
            halves = (jnp.where(low_half, xp, 0.0).astype(BF16), jnp.where(low_half, 0.0, xp).astype(BF16))
            acc = y_off[:, p * LANES:(p + 1) * LANES] + xp * dsk_ref[:, g * gw + p * LANES:g * gw + (p + 1) * LANES]
            for r in range(LANES // SSD_HEAD_DIM):
                hd = g * hpg + p * (LANES // SSD_HEAD_DIM) + r
                decay = jnp.exp(jnp.where(tril, a_cs[:, hd:hd + 1] - a_cs_t[hd:hd + 1, :], NEG_INF))
                scores = (cb * decay * dt_t[hd:hd + 1, :]).astype(BF16)
                acc = acc + _dot(scores, halves[r])
            yacc_ref[:, g * gw + p * LANES:g * gw + (p + 1) * LANES] = acc

    yz = yacc_ref[...] * _silu(z_ref[...])
    y_ref[...] = (yz * _rms_scale(yz) * nw_ref[...]).astype(BF16)


def _ssd(xbc, dt, z, cw, cb, dtb, a_neg, dsk, nw, batch, seq):
    t, width = z.shape
    cs = SSD_CHUNK
    nc = seq // cs
    heads = width // SSD_HEAD_DIM
    tri = (jnp.arange(cs)[:, None] >= jnp.arange(cs)[None, :]).astype(BF16)
    expand = (jnp.arange(LANES)[:, None] == (jnp.arange(width)[None, :] // SSD_HEAD_DIM)).astype(BF16)
    pad_h = lambda v: jnp.pad(v.astype(F32), (0, LANES - heads))[None, :]
    blk = lambda n: pl.BlockSpec((cs, n), lambda b, c: (b * nc + c, 0))
    return pl.pallas_call(
        functools.partial(_ssd_body, width=width),
        grid=(batch, nc),
        in_specs=[blk(xbc.shape[1]), blk(LANES), blk(width),
                  _const_spec(cw.shape), _const_spec((1, xbc.shape[1])), _const_spec((1, LANES)),
                  _const_spec((1, LANES)), _const_spec((1, width)), _const_spec((1, width)),
                  _const_spec((cs, cs)), _const_spec((LANES, width))],
        out_specs=blk(width),
        out_shape=jax.ShapeDtypeStruct((t, width), BF16),
        scratch_shapes=[pltpu.VMEM((cs + 8, xbc.shape[1]), F32),
                        pltpu.VMEM((SSD_GROUPS, SSD_STATE, width // SSD_GROUPS), F32),
                        pltpu.VMEM((cs, width), F32)],
        compiler_params=_params(("parallel", "arbitrary")),
        name="ssd",
    )(xbc, dt, z, cw, cb[None, :], pad_h(dtb), pad_h(a_neg),
      jnp.repeat(dsk.astype(F32), SSD_HEAD_DIM)[None, :], nw[None, :], tri, expand)


def _rel_bias_tile(tab_ref, head, n_heads, delta, tq, tk):
    row = lax.broadcasted_iota(jnp.int32, (tq, tk), 0)
    col = lax.broadcasted_iota(jnp.int32, (tq, tk), 1)
    n = jnp.maximum(delta + row - col, 0)
    max_exact = REL_BUCKETS // 2
    large = max_exact + (jnp.log(jnp.maximum(n, 1).astype(F32) / max_exact)
                         / math.log(REL_MAX_DIST / max_exact) * (REL_BUCKETS - max_exact)).astype(jnp.int32)
    bucket = jnp.where(n < max_exact, n, jnp.minimum(large, REL_BUCKETS - 1))
    far = tab_ref[(REL_BUCKETS - 1) * n_heads + head]
    bias = jnp.zeros((tq, tk), F32)
    for j in range(REL_BUCKETS - 1):
        bias = jnp.where(bucket == j, tab_ref[j * n_heads + head] - far, bias)
    return bias


def _attn_body(tab_ref, lam_ref, q_ref, k_ref, v_ref, g_ref, sw_ref, o_ref, bias_ref, *, n_heads, seq):
    tq = tk = ATTN_BLOCK
    nq = seq // tq
    head = pl.program_id(0)

    @pl.when(pl.program_id(1) == 0)
    def _():
        for i, delta in enumerate((0, tq)):
            b = _rel_bias_tile(tab_ref, head, n_heads, delta, tq, tk)
            bias_ref[i] = jnp.concatenate([b, b], axis=0)

    lam = lam_ref[0]
    out_scale = lam_ref[1]
    lane = lax.broadcasted_iota(jnp.int32, (tq, DA_V_DIM), 1)
    first_map = lane < DA_QK_DIM
    row = lax.broadcasted_iota(jnp.int32, (2 * tq, tk), 0)
    col = lax.broadcasted_iota(jnp.int32, (2 * tq, tk), 1)
    causal = jnp.where(row >= tq, row - tq, row) >= col

    def tile(qst, k0, carry, bias, masked):
        m, l, acc = carry
        s = _dot_nt(qst, k_ref[pl.ds(k0, tk), :])
        if bias is not None:
            s = s + bias
        if masked:
            s = jnp.where(causal, s, NEG_INF)
        m_new = jnp.maximum(m, jnp.max(s, axis=-1, keepdims=True))
        alpha = jnp.exp(m - m_new)
        p = jnp.exp(s - m_new)
        l = alpha * l + jnp.sum(p, axis=-1, keepdims=True)
        acc = alpha * acc + _dot(p.astype(BF16), v_ref[pl.ds(k0, tk), :])
        return m_new, l, acc

    def q_block(qi, q0):
        qb = q_ref[pl.ds(q0, tq), :]
        zero = jnp.zeros_like(qb)
        qst = jnp.concatenate([jnp.where(first_map, qb, zero), jnp.where(first_map, zero, qb)], axis=0)
        carry = (jnp.full((2 * tq, 1), NEG_INF, F32), jnp.zeros((2 * tq, 1), F32),
                 jnp.zeros((2 * tq, DA_V_DIM), F32))
        if qi is not None:
            carry = lax.fori_loop(
                0, qi - 1, lambda ki, cr: tile(qst, pl.multiple_of(ki * tk, tk), cr, None, False), carry)
            carry = tile(qst, pl.multiple_of(q0 - tk, tk), carry, bias_ref[1], False)
        _, l, acc = tile(qst, q0, carry, bias_ref[0], True)
        o = acc[:tq] / l[:tq] - lam * (acc[tq:] / l[tq:])
        o = o * _rms_scale(o) * sw_ref[...] * out_scale
        o_ref[pl.ds(q0, tq), :] = (o * _silu(g_ref[pl.ds(q0, tq), :])).astype(BF16)

    q_block(None, 0)

    def later_block(qi, _):
        q_block(qi, pl.multiple_of(qi * tq, tq))
        return 0

    lax.fori_loop(1, nq, later_block, 0)


def _diff_attn(q, k, v, g, table, lam_pair, sw, batch, seq):
    t, width = v.shape
    n_heads = width // DA_V_DIM
    blk = pl.BlockSpec((seq, DA_V_DIM), lambda h, b: (b, h))
    smem = pl.BlockSpec(memory_space=pltpu.SMEM)
    return pl.pallas_call(
        functools.partial(_attn_body, n_heads=n_heads, seq=seq),
        grid=(n_heads, batch),
        in_specs=[smem, smem, blk, blk, blk, blk, _const_spec((1, DA_V_DIM))],
        out_specs=blk,
        out_shape=jax.ShapeDtypeStruct((t, width), BF16),
        scratch_shapes=[pltpu.VMEM((2, 2 * ATTN_BLOCK, ATTN_BLOCK), F32)],
        compiler_params=_params(("arbitrary", "arbitrary")),
        name="diff_attn",
    )(table.reshape(-1), lam_pair, q, k, v, g, sw[None, :])


def _hgrn_head(q, k, lf, v, st_ref, hi, tri):
    c, dk = q.shape
    b = _dot_exact_lhs(tri, lf, 3)
    b_last = b[c - 1:c, :]
    s_t = st_ref[hi]
    o = _dot_nt((q * jnp.exp(b)).astype(BF16), s_t.astype(BF16))
    kd = (k * jnp.exp(b_last - b)).astype(BF16)
    st_ref[hi] = s_t * jnp.exp(b_last) + _dot_tn(v, kd)

    row = lax.broadcasted_iota(jnp.int32, (c, dk), 0)
    colm = lax.broadcasted_iota(jnp.int32, (c, c), 1)
    rowm = lax.broadcasted_iota(jnp.int32, (c, c), 0)
    att = jnp.zeros((c, c), F32)

    m = c // 2
    while m >= HG_DIRECT:
        ref_rows = jnp.concatenate(
            [jnp.broadcast_to(b[u + m - 1:u + m, :], (2 * m, dk)) for u in range(0, c, 2 * m)], axis=0)
        is_query = (row & m) != 0
        e = jnp.exp(jnp.where(is_query, b - ref_rows, ref_rows - b))
        qz = jnp.where(is_query, q * e, 0.0).astype(BF16)
        kz = jnp.where(is_query, 0.0, k * e).astype(BF16)
        same_group = (rowm & -(2 * m)) == (colm & -(2 * m))
        att = att + jnp.where(same_group, _dot_nt(qz, kz), 0.0)
        m //= 2

    att = att + jnp.where(rowm == colm, jnp.sum(q * k, axis=-1, keepdims=True), 0.0)
    for d in range(1, HG_DIRECT):
        valid = (row & (HG_DIRECT - 1)) >= d
        e = jnp.exp(jnp.where(valid, b - pltpu.roll(b, d, 0), NEG_INF))
        pair = jnp.sum(q * pltpu.roll(k, d, 0) * e, axis=-1, keepdims=True)
        att = att + jnp.where(rowm - d == colm, pair, 0.0)

    return o + _dot(att.astype(BF16), v)


def _hgrn_body(q_ref, k_ref, lf_ref, v_ref, g_ref, nw_ref, tri_ref, y_ref, st_ref):
    @pl.when(pl.program_id(2) == 0)
    def _():
        st_ref[...] = jnp.zeros(st_ref.shape, F32)

    tri = tri_ref[...]
    for hi in range(HG_HEADS_PER_STEP):
        hs = slice(hi * HG_HEAD_DIM, (hi + 1) * HG_HEAD_DIM)
        o = _hgrn_head(q_ref[:, hs], k_ref[:, hs], lf_ref[:, hs], v_ref[:, hs], st_ref, hi, tri)
        o = o * _rms_scale(o) * nw_ref[...]
        y_ref[:, hs] = (o * _silu(g_ref[:, hs])).astype(BF16)


def _hgrn(q, k, lf, v, g, nw, batch, seq):
    t, width = q.shape
    c = HG_CHUNK
    nc = seq // c
    bw = HG_HEADS_PER_STEP * HG_HEAD_DIM
    tri = (jnp.arange(c)[:, None] >= jnp.arange(c)[None, :]).astype(BF16)
    blk = pl.BlockSpec((c, bw), lambda b, hg, ci: (b * nc + ci, hg))
    return pl.pallas_call(
        _hgrn_body,
        grid=(batch, width // bw, nc),
        in_specs=[blk, blk, blk, blk, blk, _const_spec((1, HG_HEAD_DIM)), _const_spec((c, c))],
        out_specs=blk,
        out_shape=jax.ShapeDtypeStruct((t, width), BF16),
        scratch_shapes=[pltpu.VMEM((HG_HEADS_PER_STEP, HG_HEAD_DIM, HG_HEAD_DIM), F32)],
        compiler_params=_params(("parallel", "parallel", "arbitrary")),
        name="hgrn",
    )(q, k, lf, v, g, nw[None, :], tri)


def kernel(x, norm_w, final_norm_w, rel_bias, even_w_in, even_w_out, conv_w, conv_b, dt_bias, A_log, D_skip,
           ssd_norm_w, lambda_q1, lambda_k1, lambda_q2, lambda_k2, subln_w, odd_w_in, odd_w_out,
           hgrn_lower_bounds, hgrn_norm_w):
    batch, seq, d = x.shape
    depth = norm_w.shape[0]
    tm = PROJ_ROWS
    h = x.reshape(batch * seq, d)

    heads = A_log.shape[1]
    ssd_w = heads * SSD_HEAD_DIM
    conv_ch = conv_w.shape[2]
    da_w = rel_bias.shape[1] * DA_V_DIM
    offs = [0]
    for n in (ssd_w, conv_ch, heads, da_w, da_w, da_w, da_w):
        offs.append(offs[-1] + n)

    lb_all = jax.nn.softmax(hgrn_lower_bounds.astype(F32), axis=0)
    lb_all = jnp.cumsum(lb_all, axis=0) - lb_all[0]

    for layer in range(depth):
        nw = norm_w[layer][None, :]
        last = layer == depth - 1
        if layer % 2 == 0:
            e = layer // 2
            w = even_w_in[e]
            seg = lambda i: w[:, offs[i]:offs[i + 1]]
            w_perm = jnp.concatenate(
                [seg(0), seg(1), seg(3) * (DA_QK_DIM ** -0.5), seg(4), seg(5), seg(6),
                 jnp.pad(seg(2), ((0, 0), (0, LANES - heads)))], axis=1).astype(BF16)
            z, xbc, q, k, v, g, dt = _proj_even(h, nw, w_perm, tm)
            y_a = _ssd(xbc, dt, z, conv_w[e], conv_b[e], dt_bias[e], -jnp.exp(A_log[e].astype(F32)),
                       D_skip[e], ssd_norm_w[e], batch, seq)
            lam_init = 0.8 - 0.6 * math.exp(-0.3 * layer)
            lam = (jnp.exp(jnp.sum(lambda_q1[e].astype(F32) * lambda_k1[e].astype(F32)))
                   - jnp.exp(jnp.sum(lambda_q2[e].astype(F32) * lambda_k2[e].astype(F32))) + lam_init)
            lam_pair = jnp.stack([lam, jnp.asarray(1.0 - lam_init, F32)])
            y_b = _diff_attn(q, k, v, g, rel_bias.astype(F32), lam_pair, subln_w[e], batch, seq)
            w_out = even_w_out[e].astype(BF16)
            h = _out_proj(h, [y_a, y_b], [w_out[:ssd_w], w_out[ssd_w:]], tm,
                          final_norm_w[None, :] if last else None)
        else:
            o = layer // 2
            q, k, lf, v, g = _proj_odd(h, nw, odd_w_in[o].astype(BF16), lb_all[layer][None, :], tm // 2)
            y = _hgrn(q, k, lf, v, g, hgrn_norm_w[o], batch, seq)
            h = _out_proj(h, [y], [odd_w_out[o].astype(BF16)], tm, final_norm_w[None, :] if last else None)
    return h.reshape(batch, seq, d)
```

```python
import functools
import math

import jax
import jax.numpy as jnp
from jax import lax
from jax.experimental import pallas as pl
from jax.experimental.pallas import tpu as pltpu

F32 = jnp.float32
BF16 = jnp.bfloat16
EPS = 1e-6
NEG_INF = float("-inf")

LANES = 128
SSD_HEAD_DIM = 64
SSD_GROUPS = 4
SSD_STATE = 128
SSD_CONV = 4
SSD_CHUNK = 128
DA_QK_DIM = 64
DA_V_DIM = 128
REL_BUCKETS = 32
REL_MAX_DIST = 128
HG_HEAD_DIM = 128
HG_CHUNK = 128
HG_HEADS_PER_STEP = 4
ATTN_BLOCK = 256
ATTN_BATCH = 4
PROJ_ROWS = 512
VMEM_LIMIT = 56 * 1024 * 1024


def _dot(a, b):
    return jnp.dot(a, b, preferred_element_type=F32)


def _dot_nt(a, b):
    return lax.dot_general(a, b, (((1,), (1,)), ((), ())), preferred_element_type=F32)


def _dot_tn(a, b):
    return lax.dot_general(a, b, (((0,), (0,)), ((), ())), preferred_element_type=F32)


def _split_bf16(x, parts):
    out = []
    r = x
    for _ in range(parts):
        p = r.astype(BF16)
        out.append(p)
        r = r - p.astype(F32)
    return out


def _dot_exact_rhs(x, m_bf16, parts):
    return sum(_dot(p, m_bf16) for p in _split_bf16(x, parts))


def _dot_exact_lhs(m_bf16, x, parts):
    return sum(_dot(m_bf16, p) for p in _split_bf16(x, parts))


def _sigmoid(x):
    return 1.0 / (1.0 + jnp.exp(-x))


def _silu(x):
    return x * _sigmoid(x)


def _rms_scale(x):
    return lax.rsqrt(jnp.mean(x * x, axis=-1, keepdims=True) + EPS)


def _const_spec(shape):
    nd = len(shape)
    return pl.BlockSpec(shape, lambda *_: (0,) * nd, pipeline_mode=pl.Buffered(1))


def _params(sem):
    return pltpu.CompilerParams(dimension_semantics=sem, vmem_limit_bytes=VMEM_LIMIT)


def _proj_even_body(h_ref, nw_ref, w_ref, wt_ref, z_ref, xbc_ref, k_ref, g_ref, dt_ref, qt_ref, vt_ref):
    x = h_ref[...]
    u = (x * _rms_scale(x) * nw_ref[...]).astype(BF16)
    col = 0
    for ref in (z_ref, xbc_ref, k_ref, g_ref, dt_ref):
        n = ref.shape[1]
        step = min(n, 512)
        for j in range(0, n, step):
            ref[:, j:j + step] = _dot(u, w_ref[:, col + j:col + j + step]).astype(ref.dtype)
        col += n
    row = 0
    for ref in (qt_ref, vt_ref):
        n = ref.shape[0]
        for j in range(0, n, 256):
            ref[j:j + 256, :] = _dot_nt(wt_ref[row + j:row + j + 256, :], u).astype(ref.dtype)
        row += n


def _proj_even(h, nw, w, wt, tm):
    t, d = h.shape
    widths = (d, 2 * d, d, d, LANES)
    dtypes = (F32, F32, BF16, F32, F32)
    row = lambda n: pl.BlockSpec((tm, n), lambda i: (i, 0))
    col = pl.BlockSpec((d, tm), lambda i: (0, i))
    return pl.pallas_call(
        _proj_even_body,
        grid=(t // tm,),
        in_specs=[row(d), _const_spec((1, d)), _const_spec(w.shape), _const_spec(wt.shape)],
        out_specs=[row(n) for n in widths] + [col, col],
        out_shape=[jax.ShapeDtypeStruct((t, n), dt) for n, dt in zip(widths, dtypes)]
        + [jax.ShapeDtypeStruct((d, t), BF16)] * 2,
        compiler_params=_params(("parallel",)),
        name="proj_even",
    )(h, nw, w, wt)


def _proj_odd_body(h_ref, nw_ref, w_ref, lb_ref, q_ref, k_ref, lf_ref, v_ref, g_ref):
    x = h_ref[...]
    u = (x * _rms_scale(x) * nw_ref[...]).astype(BF16)
    n = q_ref.shape[1]
    step = 512
    for j in range(0, n, step):
        cs = slice(j, j + step)
        q_ref[:, cs] = _silu(_dot(u, w_ref[:, j:j + step]))
        lb = lb_ref[:, cs]
        f = lb + (1.0 - lb) * _sigmoid(_dot(u, w_ref[:, n + j:n + j + step]))
        lf_ref[:, cs] = jnp.log(f)
        k_ref[:, cs] = 1.0 - f
        v_ref[:, cs] = _dot(u, w_ref[:, 2 * n + j:2 * n + j + step]).astype(BF16)
        g_ref[:, cs] = _dot(u, w_ref[:, 3 * n + j:3 * n + j + step])


def _proj_odd(h, nw, w, lb, tm):
    t, d = h.shape
    n = w.shape[1] // 4
    dtypes = (F32, F32, F32, BF16, F32)
    row = lambda m: pl.BlockSpec((tm, m), lambda i: (i, 0))
    return pl.pallas_call(
        _proj_odd_body,
        grid=(t // tm,),
        in_specs=[row(d), _const_spec((1, d)), _const_spec(w.shape), _const_spec((1, n))],
        out_specs=[row(n) for _ in dtypes],
        out_shape=[jax.ShapeDtypeStruct((t, n), dt) for dt in dtypes],
        compiler_params=_params(("parallel",)),
        name="proj_odd",
    )(h, nw, w, lb)


def _out_proj_body(*refs, n_act, final):
    h_ref = refs[0]
    acts = refs[1:1 + n_act]
    ws = refs[1 + n_act:1 + 2 * n_act]
    rest = refs[1 + 2 * n_act:]
    acc = h_ref[...]
    for a_ref, w_ref in zip(acts, ws):
        acc = acc + _dot(a_ref[...], w_ref[...])
    if final:
        fw_ref, o_ref = rest
        acc = acc * _rms_scale(acc) * fw_ref[...]
    else:
        (o_ref,) = rest
    o_ref[...] = acc


def _out_proj(h, acts, ws, tm, final_w=None):
    t, d = h.shape
    final = final_w is not None
    row = lambda n: pl.BlockSpec((tm, n), lambda i: (i, 0))
    in_specs = [row(d)] + [row(a.shape[1]) for a in acts] + [_const_spec(w.shape) for w in ws]
    args = [h, *acts, *ws]
    if final:
        in_specs.append(_const_spec((1, d)))
        args.append(final_w)
    return pl.pallas_call(
        functools.partial(_out_proj_body, n_act=len(acts), final=final),
        grid=(t // tm,),
        in_specs=in_specs,
        out_specs=row(d),
        out_shape=jax.ShapeDtypeStruct((t, d), F32),
        compiler_params=_params(("parallel",)),
        name="out_proj",
    )(*args)


def _softplus(x):
    return jnp.maximum(x, 0.0) + jnp.log(1.0 + jnp.exp(-jnp.abs(x)))


def _ssd_body(xbc_ref, dt_ref, z_ref, cw_ref, cb_ref, dtb_ref, a_ref, dsk_ref, nw_ref, tri_ref,
              exp_ref, y_ref, ext_ref, st_ref, yacc_ref, *, width):
    cs = SSD_CHUNK
    gw = width // SSD_GROUPS
    hpg = gw // SSD_HEAD_DIM
    pad = 8
    c = pl.program_id(1)

    @pl.when(c == 0)
    def _():
        ext_ref[0:pad, :] = jnp.zeros((pad, ext_ref.shape[1]), F32)
        st_ref[...] = jnp.zeros(st_ref.shape, F32)

    xin = xbc_ref[...]
    ext_ref[pad:pad + cs, :] = xin
    conv = cb_ref[...] + cw_ref[SSD_CONV - 1:SSD_CONV, :] * xin
    for j in range(1, SSD_CONV):
        conv = conv + cw_ref[SSD_CONV - 1 - j:SSD_CONV - j, :] * ext_ref[pad - j:pad - j + cs, :]
    ext_ref[0:pad, :] = xin[cs - pad:cs, :]
    act = _silu(conv)

    dt = _softplus(dt_ref[...] + dtb_ref[...])
    a = dt * a_ref[...]
    a_cs = _dot_exact_lhs(tri_ref[...], a, 3)
    a_cs_t = a_cs.T
    dt_t = dt.T
    a_last = a_cs[cs - 1:cs, :]
    per_head = jnp.concatenate(
        [dt * jnp.exp(a_last - a_cs), jnp.exp(a_cs), jnp.broadcast_to(jnp.exp(a_last), (8, LANES))], axis=0)
    per_col = _dot_exact_rhs(per_head, exp_ref[...], 2)
    w_state = per_col[0:cs]
    e_acs = per_col[cs:2 * cs]
    e_last = per_col[2 * cs:2 * cs + 1]

    row = lax.broadcasted_iota(jnp.int32, (cs, cs), 0)
    colm = lax.broadcasted_iota(jnp.int32, (cs, cs), 1)
    tril = row >= colm
    lane = lax.broadcasted_iota(jnp.int32, (cs, LANES), 1)
    low_half = lane < SSD_HEAD_DIM

    for g in range(SSD_GROUPS):
        gs = slice(g * gw, (g + 1) * gw)
        xg = act[:, gs]
        bg = act[:, width + g * SSD_STATE:width + (g + 1) * SSD_STATE].astype(BF16)
        cg = act[:, width + (SSD_GROUPS + g) * SSD_STATE:width + (SSD_GROUPS + g + 1) * SSD_STATE].astype(BF16)
        cb = _dot_nt(cg, bg)
        s_prev = st_ref[g]
        y_off = _dot(cg, s_prev.astype(BF16)) * e_acs[:, gs]
        xd = (xg * w_state[:, gs]).astype(BF16)
        st_ref[g] = s_prev * e_last[:, gs] + _dot_tn(bg, xd)
        for p in range(gw // LANES):
            xp = xg[:, p * LANES:(p + 1) * LANES]
            halves = (jnp.where(low_half, xp, 0.0).astype(BF16), jnp.where(low_half, 0.0, xp).astype(BF16))
            acc = y_off[:, p * LANES:(p + 1) * LANES] + xp * dsk_ref[:, g * gw + p * LANES:g * gw + (p + 1) * LANES]
            for r in range(LANES // SSD_HEAD_DIM):
                hd = g * hpg + p * (LANES // SSD_HEAD_DIM) + r
                decay = jnp.exp(jnp.where(tril, a_cs[:, hd:hd + 1] - a_cs_t[hd:hd + 1, :], NEG_INF))
                scores = (cb * decay * dt_t[hd:hd + 1, :]).astype(BF16)
                acc = acc + _dot(scores, halves[r])
            yacc_ref[:, g * gw + p * LANES:g * gw + (p + 1) * LANES] = acc

    yz = yacc_ref[...] * _silu(z_ref[...])
    y_ref[...] = (yz * _rms_scale(yz) * nw_ref[...]).astype(BF16)


def _ssd(xbc, dt, z, cw, cb, dtb, a_neg, dsk, nw, batch, seq):
    t, width = z.shape
    cs = SSD_CHUNK
    nc = seq // cs
    heads = width // SSD_HEAD_DIM
    tri = (jnp.arange(cs)[:, None] >= jnp.arange(cs)[None, :]).astype(BF16)
    expand = (jnp.arange(LANES)[:, None] == (jnp.arange(width)[None, :] // SSD_HEAD_DIM)).astype(BF16)
    pad_h = lambda v: jnp.pad(v.astype(F32), (0, LANES - heads))[None, :]
    blk = lambda n: pl.BlockSpec((cs, n), lambda b, c: (b * nc + c, 0))
    return pl.pallas_call(
        functools.partial(_ssd_body, width=width),
        grid=(batch, nc),
        in_specs=[blk(xbc.shape[1]), blk(LANES), blk(width),
                  _const_spec(cw.shape), _const_spec((1, xbc.shape[1])), _const_spec((1, LANES)),
                  _const_spec((1, LANES)), _const_spec((1, width)), _const_spec((1, width)),
                  _const_spec((cs, cs)), _const_spec((LANES, width))],
        out_specs=blk(width),
        out_shape=jax.ShapeDtypeStruct((t, width), BF16),
        scratch_shapes=[pltpu.VMEM((cs + 8, xbc.shape[1]), F32),
                        pltpu.VMEM((SSD_GROUPS, SSD_STATE, width // SSD_GROUPS), F32),
                        pltpu.VMEM((cs, width), F32)],
        compiler_params=_params(("parallel", "arbitrary")),
        name="ssd",
    )(xbc, dt, z, cw, cb[None, :], pad_h(dtb), pad_h(a_neg),
      jnp.repeat(dsk.astype(F32), SSD_HEAD_DIM)[None, :], nw[None, :], tri, expand)


def _rel_bias_tile(tab_ref, head, n_heads, delta, tk, tq):
    key = lax.broadcasted_iota(jnp.int32, (tk, tq), 0)
    qry = lax.broadcasted_iota(jnp.int32, (tk, tq), 1)
    n = jnp.maximum(delta + qry - key, 0)
    max_exact = REL_BUCKETS // 2
    large = max_exact + (jnp.log(jnp.maximum(n, 1).astype(F32) / max_exact)
                         / math.log(REL_MAX_DIST / max_exact) * (REL_BUCKETS - max_exact)).astype(jnp.int32)
    bucket = jnp.where(n < max_exact, n, jnp.minimum(large, REL_BUCKETS - 1))
    far = tab_ref[(REL_BUCKETS - 1) * n_heads + head]
    bias = jnp.zeros((tk, tq), F32)
    for j in range(REL_BUCKETS - 1):
        bias = jnp.where(bucket == j, tab_ref[j * n_heads + head] - far, bias)
    return bias


def _attn_body(tab_ref, lam_ref, qt_ref, k_ref, vt_ref, g_ref, sw_ref, o_ref, bias_ref, *, n_heads, seq):
    tq = tk = ATTN_BLOCK
    nq = seq // tq
    head = pl.program_id(0)

    @pl.when(pl.program_id(1) == 0)
    def _():
        for i, delta in enumerate((0, tq)):
            b = _rel_bias_tile(tab_ref, head, n_heads, delta, tk, tq)
            bias_ref[i] = jnp.concatenate([b, b], axis=1)

    lam = lam_ref[0]
    out_scale = lam_ref[1]
    feat = lax.broadcasted_iota(jnp.int32, (DA_V_DIM, tq), 0)
    first_map = feat < DA_QK_DIM
    key = lax.broadcasted_iota(jnp.int32, (tk, 2 * tq), 0)
    qry = lax.broadcasted_iota(jnp.int32, (tk, 2 * tq), 1)
    causal = jnp.where(qry >= tq, qry - tq, qry) >= key

    def scores(base, qst, k0, bias, masked):
        s = _dot(k_ref[pl.ds(base + k0, tk), :], qst)
        if bias is not None:
            s = s + bias
        if masked:
            s = jnp.where(causal, s, NEG_INF)
        return s

    def update(base, s, k0, carry):
        m, l, acc = carry
        m_new = jnp.maximum(m, jnp.max(s, axis=0, keepdims=True))
        alpha = jnp.exp(m - m_new)
        p = jnp.exp(s - m_new)
        l = alpha * l + jnp.sum(p, axis=0, keepdims=True)
        acc = alpha * acc + _dot(vt_ref[:, pl.ds(base + k0, tk)], p.astype(BF16))
        return m_new, l, acc

    def q_block(qi, q0):
        bases = [bi * seq for bi in range(ATTN_BATCH)]
        qsts = []
        for base in bases:
            qb = qt_ref[:, pl.ds(base + q0, tq)]
            zero = jnp.zeros_like(qb)
            qsts.append(jnp.concatenate([jnp.where(first_map, qb, zero), jnp.where(first_map, zero, qb)], axis=1))
        init = (jnp.full((1, 2 * tq), NEG_INF, F32), jnp.zeros((1, 2 * tq), F32),
                jnp.zeros((DA_V_DIM, 2 * tq), F32))

        def all_chains(k0, carries, bias, masked):
            ss = [scores(base, qst, k0, bias, masked) for base, qst in zip(bases, qsts)]
            return tuple(update(base, s, k0, cr) for base, s, cr in zip(bases, ss, carries))

        carries = (init,) * ATTN_BATCH
        if qi is not None:
            carries = lax.fori_loop(
                0, qi - 1, lambda ki, cr: all_chains(pl.multiple_of(ki * tk, tk), cr, None, False), carries)
            carries = all_chains(pl.multiple_of(q0 - tk, tk), carries, bias_ref[1], False)
        carries = all_chains(q0, carries, bias_ref[0], True)
        for base, (_, l, acc) in zip(bases, carries):
            o_t = acc[:, :tq] / l[:, :tq] - lam * (acc[:, tq:] / l[:, tq:])
            o = o_t.T
            o = o * _rms_scale(o) * sw_ref[...] * out_scale
            o_ref[pl.ds(base + q0, tq), :] = (o * _silu(g_ref[pl.ds(base + q0, tq), :])).astype(BF16)

    q_block(None, 0)

    def later_block(qi, _):
        q_block(qi, pl.multiple_of(qi * tq, tq))
        return 0

    lax.fori_loop(1, nq, later_block, 0)


def _diff_attn(qt, k, vt, g, table, lam_pair, sw, batch, seq):
    t, width = k.shape
    n_heads = width // DA_V_DIM
    blk = pl.BlockSpec((ATTN_BATCH * seq, DA_V_DIM), lambda h, b: (b, h))
    blk_t = pl.BlockSpec((DA_V_DIM, ATTN_BATCH * seq), lambda h, b: (h, b))
    smem = pl.BlockSpec(memory_space=pltpu.SMEM)
    return pl.pallas_call(
        functools.partial(_attn_body, n_heads=n_heads, seq=seq),
        grid=(n_heads, batch // ATTN_BATCH),
        in_specs=[smem, smem, blk_t, blk, blk_t, blk, _const_spec((1, DA_V_DIM))],
        out_specs=blk,
        out_shape=jax.ShapeDtypeStruct((t, width), BF16),
        scratch_shapes=[pltpu.VMEM((2, ATTN_BLOCK, 2 * ATTN_BLOCK), F32)],
        compiler_params=_params(("arbitrary", "arbitrary")),
        name="diff_attn",
    )(table.reshape(-1), lam_pair, qt, k, vt, g, sw[None, :])


def _hgrn_head(q, k, b, v, st_ref, hi, row, lvl):
    c, dk = q.shape
    b_last = b[c - 1:c, :]
    s_t = st_ref[hi]
    o = _dot_nt((q * jnp.exp(b)).astype(BF16), s_t.astype(BF16))
    kd = (k * jnp.exp(b_last - b)).astype(BF16)
    st_ref[hi] = s_t * jnp.exp(b_last) + _dot_tn(v, kd)

    att = jnp.where(lvl == 0, jnp.sum(q * k, axis=-1, keepdims=True), 0.0)
    m = c // 2
    while m >= 1:
        ref_rows = jnp.concatenate(
            [jnp.broadcast_to(b[u + m - 1:u + m, :], (2 * m, dk)) for u in range(0, c, 2 * m)], axis=0)
        is_query = (row & m) != 0
        d = b - ref_rows
        z = (jnp.where(is_query, q, k) * jnp.exp(jnp.where(is_query, d, -d))).astype(BF16)
        att = jnp.where(lvl == m, _dot_nt(z, z), att)
        m //= 2
    return o + _dot(att.astype(BF16), v)


def _hgrn_body(q_ref, k_ref, lf_ref, v_ref, g_ref, nw_ref, tri_ref, lvl_ref, y_ref, st_ref):
    @pl.when(pl.program_id(2) == 0)
    def _():
        st_ref[...] = jnp.zeros(st_ref.shape, F32)

    c = q_ref.shape[0]
    lf_hi, lf_lo = _split_bf16(lf_ref[...], 2)
    b_all = _dot(tri_ref[...], jnp.concatenate([lf_hi, lf_lo], axis=0))
    row = lax.broadcasted_iota(jnp.int32, (c, HG_HEAD_DIM), 0)
    lvl = lvl_ref[...]
    for hi in range(HG_HEADS_PER_STEP):
        hs = slice(hi * HG_HEAD_DIM, (hi + 1) * HG_HEAD_DIM)
        o = _hgrn_head(q_ref[:, hs], k_ref[:, hs], b_all[:, hs], v_ref[:, hs], st_ref, hi, row, lvl)
        o = o * _rms_scale(o) * nw_ref[...]
        y_ref[:, hs] = (o * _silu(g_ref[:, hs])).astype(BF16)


def _hgrn(q, k, lf, v, g, nw, batch, seq):
    t, width = q.shape
    c = HG_CHUNK
    nc = seq // c
    bw = HG_HEADS_PER_STEP * HG_HEAD_DIM
    idx = jnp.arange(c)
    tri = (idx[:, None] >= idx[None, :]).astype(BF16)
    tri2 = jnp.concatenate([tri, tri], axis=1)
    diff = idx[:, None] ^ idx[None, :]
    top_bit = jnp.left_shift(1, jnp.maximum(31 - lax.clz(diff), 0))
    lvl = jnp.where(idx[:, None] > idx[None, :], top_bit, jnp.where(diff == 0, 0, -1)).astype(jnp.int32)
    blk = pl.BlockSpec((c, bw), lambda b, hg, ci: (b * nc + ci, hg))
    return pl.pallas_call(
        _hgrn_body,
        grid=(batch, width // bw, nc),
        in_specs=[blk, blk, blk, blk, blk, _const_spec((1, HG_HEAD_DIM)), _const_spec((c, 2 * c)),
                  _const_spec((c, c))],
        out_specs=blk,
        out_shape=jax.ShapeDtypeStruct((t, width), BF16),
        scratch_shapes=[pltpu.VMEM((HG_HEADS_PER_STEP, HG_HEAD_DIM, HG_HEAD_DIM), F32)],
        compiler_params=_params(("parallel", "parallel", "arbitrary")),
        name="hgrn",
    )(q, k, lf, v, g, nw[None, :], tri2, lvl)


def kernel(x, norm_w, final_norm_w, rel_bias, even_w_in, even_w_out, conv_w, conv_b, dt_bias, A_log, D_skip,
           ssd_norm_w, lambda_q1, lambda_k1, lambda_q2, lambda_k2, subln_w, odd_w_in, odd_w_out,
           hgrn_lower_bounds, hgrn_norm_w):
    batch, seq, d = x.shape
    depth = norm_w.shape[0]
    tm = PROJ_ROWS
    h = x.reshape(batch * seq, d)

    heads = A_log.shape[1]
    ssd_w = heads * SSD_HEAD_DIM
    conv_ch = conv_w.shape[2]
    da_w = rel_bias.shape[1] * DA_V_DIM
    offs = [0]
    for n in (ssd_w, conv_ch, heads, da_w, da_w, da_w, da_w):
        offs.append(offs[-1] + n)

    lb_all = jax.nn.softmax(hgrn_lower_bounds.astype(F32), axis=0)
    lb_all = jnp.cumsum(lb_all, axis=0) - lb_all[0]

    for layer in range(depth):
        nw = norm_w[layer][None, :]
        last = layer == depth - 1
        if layer % 2 == 0:
            e = layer // 2
            w = even_w_in[e]
            seg = lambda i: w[:, offs[i]:offs[i + 1]]
            w_perm = jnp.concatenate(
                [seg(0), seg(1), seg(4), seg(6), jnp.pad(seg(2), ((0, 0), (0, LANES - heads)))],
                axis=1).astype(BF16)
            w_t = jnp.concatenate([seg(3) * (DA_QK_DIM ** -0.5), seg(5)], axis=1).T.astype(BF16)
            z, xbc, k, g, dt, q_t, v_t = _proj_even(h, nw, w_perm, w_t, tm)
            y_a = _ssd(xbc, dt, z, conv_w[e], conv_b[e], dt_bias[e], -jnp.exp(A_log[e].astype(F32)),
                       D_skip[e], ssd_norm_w[e], batch, seq)
            lam_init = 0.8 - 0.6 * math.exp(-0.3 * layer)
            lam = (jnp.exp(jnp.sum(lambda_q1[e].astype(F32) * lambda_k1[e].astype(F32)))
                   - jnp.exp(jnp.sum(lambda_q2[e].astype(F32) * lambda_k2[e].astype(F32))) + lam_init)
            lam_pair = jnp.stack([lam, jnp.asarray(1.0 - lam_init, F32)])
            y_b = _diff_attn(q_t, k, v_t, g, rel_bias.astype(F32), lam_pair, subln_w[e], batch, seq)
            w_out = even_w_out[e].astype(BF16)
            h = _out_proj(h, [y_a, y_b], [w_out[:ssd_w], w_out[ssd_w:]], tm,
                          final_norm_w[None, :] if last else None)
        else:
            o = layer // 2
            q, k, lf, v, g = _proj_odd(h, nw, odd_w_in[o].astype(BF16), lb_all[layer][None, :], tm // 2)
            y = _hgrn(q, k, lf, v, g, hgrn_norm_w[o], batch, seq)
            h = _out_proj(h, [y], [odd_w_out[o].astype(BF16)], tm, final_norm_w[None, :] if last else None)
    return h.reshape(batch, seq, d)
```

```python
import functools
import math

import jax
import jax.numpy as jnp
from jax import lax
from jax.experimental import pallas as pl
from jax.experimental.pallas import tpu as pltpu

F32 = jnp.float32
BF16 = jnp.bfloat16
EPS = 1e-6
NEG_INF = float("-inf")

LANES = 128
SSD_HEAD_DIM = 64
SSD_GROUPS = 4
SSD_STATE = 128
SSD_CONV = 4
SSD_CHUNK = 128
DA_QK_DIM = 64
DA_V_DIM = 128
REL_BUCKETS = 32
REL_MAX_DIST = 128
HG_HEAD_DIM = 128
HG_CHUNK = 128
HG_HEADS_PER_STEP = 16
ATTN_BLOCK = 256
ATTN_BATCH = 4
PROJ_ROWS = 512
VMEM_LIMIT = 56 * 1024 * 1024


def _dot(a, b):
    return jnp.dot(a, b, preferred_element_type=F32)


def _dot_nt(a, b):
    return lax.dot_general(a, b, (((1,), (1,)), ((), ())), preferred_element_type=F32)


def _dot_tn(a, b):
    return lax.dot_general(a, b, (((0,), (0,)), ((), ())), preferred_element_type=F32)


def _split_bf16(x, parts):
    out = []
    r = x
    for _ in range(parts):
        p = r.astype(BF16)
        out.append(p)
        r = r - p.astype(F32)
    return out


def _dot_exact_rhs(x, m_bf16, parts):
    return sum(_dot(p, m_bf16) for p in _split_bf16(x, parts))


def _dot_exact_lhs(m_bf16, x, parts):
    return sum(_dot(m_bf16, p) for p in _split_bf16(x, parts))


def _sigmoid(x):
    return 1.0 / (1.0 + jnp.exp(-x))


def _silu(x):
    return x * _sigmoid(x)


def _rms_scale(x):
    return lax.rsqrt(jnp.mean(x * x, axis=-1, keepdims=True) + EPS)


def _const_spec(shape):
    nd = len(shape)
    return pl.BlockSpec(shape, lambda *_: (0,) * nd, pipeline_mode=pl.Buffered(1))


def _params(sem):
    return pltpu.CompilerParams(dimension_semantics=sem, vmem_limit_bytes=VMEM_LIMIT)


def _proj_even_body(h_ref, nw_ref, w_ref, wt_ref, z_ref, xbc_ref, k_ref, g_ref, dt_ref, qt_ref, vt_ref):
    x = h_ref[...]
    u = (x * _rms_scale(x) * nw_ref[...]).astype(BF16)
    col = 0
    for ref in (z_ref, xbc_ref, k_ref, g_ref, dt_ref):
        n = ref.shape[1]
        step = min(n, 512)
        for j in range(0, n, step):
            ref[:, j:j + step] = _dot(u, w_ref[:, col + j:col + j + step]).astype(ref.dtype)
        col += n
    row = 0
    for ref in (qt_ref, vt_ref):
        n = ref.shape[0]
        for j in range(0, n, 256):
            ref[j:j + 256, :] = _dot_nt(wt_ref[row + j:row + j + 256, :], u).astype(ref.dtype)
        row += n


def _proj_even(h, nw, w, wt, tm):
    t, d = h.shape
    widths = (d, 2 * d, d, d, LANES)
    dtypes = (F32, F32, BF16, F32, F32)
    row = lambda n: pl.BlockSpec((tm, n), lambda i: (i, 0))
    col = pl.BlockSpec((d, tm), lambda i: (0, i))
    return pl.pallas_call(
        _proj_even_body,
        grid=(t // tm,),
        in_specs=[row(d), _const_spec((1, d)), _const_spec(w.shape), _const_spec(wt.shape)],
        out_specs=[row(n) for n in widths] + [col, col],
        out_shape=[jax.ShapeDtypeStruct((t, n), dt) for n, dt in zip(widths, dtypes)]
        + [jax.ShapeDtypeStruct((d, t), BF16)] * 2,
        compiler_params=_params(("parallel",)),
        name="proj_even",
    )(h, nw, w, wt)


def _proj_odd_body(h_ref, nw_ref, w_ref, lb_ref, q_ref, k_ref, lf_ref, v_ref, g_ref):
    x = h_ref[...]
    u = (x * _rms_scale(x) * nw_ref[...]).astype(BF16)
    n = q_ref.shape[1]
    step = 512
    for j in range(0, n, step):
        cs = slice(j, j + step)
        q_ref[:, cs] = _silu(_dot(u, w_ref[:, j:j + step]))
        lb = lb_ref[:, cs]
        f = lb + (1.0 - lb) * _sigmoid(_dot(u, w_ref[:, n + j:n + j + step]))
        lf_ref[:, cs] = jnp.log(f)
        k_ref[:, cs] = 1.0 - f
        v_ref[:, cs] = _dot(u, w_ref[:, 2 * n + j:2 * n + j + step]).astype(BF16)
        g_ref[:, cs] = _dot(u, w_ref[:, 3 * n + j:3 * n + j + step])


def _proj_odd(h, nw, w, lb, tm):
    t, d = h.shape
    n = w.shape[1] // 4
    dtypes = (F32, F32, F32, BF16, F32)
    row = lambda m: pl.BlockSpec((tm, m), lambda i: (i, 0))
    return pl.pallas_call(
        _proj_odd_body,
        grid=(t // tm,),
        in_specs=[row(d), _const_spec((1, d)), _const_spec(w.shape), _const_spec((1, n))],
        out_specs=[row(n) for _ in dtypes],
        out_shape=[jax.ShapeDtypeStruct((t, n), dt) for dt in dtypes],
        compiler_params=_params(("parallel",)),
        name="proj_odd",
    )(h, nw, w, lb)


def _out_proj_body(*refs, n_act, final):
    h_ref = refs[0]
    acts = refs[1:1 + n_act]
    ws = refs[1 + n_act:1 + 2 * n_act]
    rest = refs[1 + 2 * n_act:]
    acc = h_ref[...]
    for a_ref, w_ref in zip(acts, ws):
        acc = acc + _dot(a_ref[...], w_ref[...])
    if final:
        fw_ref, o_ref = rest
        acc = acc * _rms_scale(acc) * fw_ref[...]
    else:
        (o_ref,) = rest
    o_ref[...] = acc


def _out_proj(h, acts, ws, tm, final_w=None):
    t, d = h.shape
    final = final_w is not None
    row = lambda n: pl.BlockSpec((tm, n), lambda i: (i, 0))
    in_specs = [row(d)] + [row(a.shape[1]) for a in acts] + [_const_spec(w.shape) for w in ws]
    args = [h, *acts, *ws]
    if final:
        in_specs.append(_const_spec((1, d)))
        args.append(final_w)
    return pl.pallas_call(
        functools.partial(_out_proj_body, n_act=len(acts), final=final),
        grid=(t // tm,),
        in_specs=in_specs,
        out_specs=row(d),
        out_shape=jax.ShapeDtypeStruct((t, d), F32),
        compiler_params=_params(("parallel",)),
        name="out_proj",
    )(*args)


def _softplus(x):
    return jnp.maximum(x, 0.0) + jnp.log(1.0 + jnp.exp(-jnp.abs(x)))


def _ssd_body(xbc_ref, dt_ref, z_ref, cw_ref, cb_ref, dtb_ref, a_ref, dsk_ref, nw_ref, tri_ref,
              exp_ref, y_ref, ext_ref, st_ref, yacc_ref, *, width):
    cs = SSD_CHUNK
    gw = width // SSD_GROUPS
    hpg = gw // SSD_HEAD_DIM
    pad = 8
    c = pl.program_id(1)

    @pl.when(c == 0)
    def _():
        ext_ref[0:pad, :] = jnp.zeros((pad, ext_ref.shape[1]), F32)
        st_ref[...] = jnp.zeros(st_ref.shape, F32)

    xin = xbc_ref[...]
    ext_ref[pad:pad + cs, :] = xin
    conv = cb_ref[...] + cw_ref[SSD_CONV - 1:SSD_CONV, :] * xin
    for j in range(1, SSD_CONV):
        conv = conv + cw_ref[SSD_CONV - 1 - j:SSD_CONV - j, :] * ext_ref[pad - j:pad - j + cs, :]
    ext_ref[0:pad, :] = xin[cs - pad:cs, :]
    act = _silu(conv)

    dt = _softplus(dt_ref[...] + dtb_ref[...])
    a = dt * a_ref[...]
    a_cs = _dot_exact_lhs(tri_ref[...], a, 3)
    a_cs_t = a_cs.T
    dt_t = dt.T
    a_last = a_cs[cs - 1:cs, :]
    per_head = jnp.concatenate(
        [dt * jnp.exp(a_last - a_cs), jnp.exp(a_cs), jnp.broadcast_to(jnp.exp(a_last), (8, LANES))], axis=0)
    per_col = _dot_exact_rhs(per_head, exp_ref[...], 2)
    w_state = per_col[0:cs]
    e_acs = per_col[cs:2 * cs]
    e_last = per_col[2 * cs:2 * cs + 1]

    row = lax.broadcasted_iota(jnp.int32, (cs, cs), 0)
    colm = lax.broadcasted_iota(jnp.int32, (cs, cs), 1)
    tril = row >= colm
    lane = lax.broadcasted_iota(jnp.int32, (cs, LANES), 1)
    low_half = lane < SSD_HEAD_DIM

    for g in range(SSD_GROUPS):
        gs = slice(g * gw, (g + 1) * gw)
        xg = act[:, gs]
        bg = act[:, width + g * SSD_STATE:width + (g + 1) * SSD_STATE].astype(BF16)
        cg = act[:, width + (SSD_GROUPS + g) * SSD_STATE:width + (SSD_GROUPS + g + 1) * SSD_STATE].astype(BF16)
        cb = _dot_nt(cg, bg)
        s_prev = st_ref[g]
        y_off = _dot(cg, s_prev.astype(BF16)) * e_acs[:, gs]
        xd = (xg * w_state[:, gs]).astype(BF16)
        st_ref[g] = s_prev * e_last[:, gs] + _dot_tn(bg, xd)
        for p in range(gw // LANES):
            xp = xg[:, p * LANES:(p + 1) * LANES]
            halves = (jnp.where(low_half, xp, 0.0).astype(BF16), jnp.where(low_half, 0.0, xp).astype(BF16))
            acc = y_off[:, p * LANES:(p + 1) * LANES] + xp * dsk_ref[:, g * gw + p * LANES:g * gw + (p + 1) * LANES]
            for r in range(LANES // SSD_HEAD_DIM):
                hd = g * hpg + p * (LANES // SSD_HEAD_DIM) + r
                decay = jnp.exp(jnp.where(tril, a_cs[:, hd:hd + 1] - a_cs_t[hd:hd + 1, :], NEG_INF))
                scores = (cb * decay * dt_t[hd:hd + 1, :]).astype(BF16)
                acc = acc + _dot(scores, halves[r])
            yacc_ref[:, g * gw + p * LANES:g * gw + (p + 1) * LANES] = acc

    yz = yacc_ref[...] * _silu(z_ref[...])
    y_ref[...] = (yz * _rms_scale(yz) * nw_ref[...]).astype(BF16)


def _ssd(xbc, dt, z, cw, cb, dtb, a_neg, dsk, nw, batch, seq):
    t, width = z.shape
    cs = SSD_CHUNK
    nc = seq // cs
    heads = width // SSD_HEAD_DIM
    tri = (jnp.arange(cs)[:, None] >= jnp.arange(cs)[None, :]).astype(BF16)
    expand = (jnp.arange(LANES)[:, None] == (jnp.arange(width)[None, :] // SSD_HEAD_DIM)).astype(BF16)
    pad_h = lambda v: jnp.pad(v.astype(F32), (0, LANES - heads))[None, :]
    blk = lambda n: pl.BlockSpec((cs, n), lambda b, c: (b * nc + c, 0))
    return pl.pallas_call(
        functools.partial(_ssd_body, width=width),
        grid=(batch, nc),
        in_specs=[blk(xbc.shape[1]), blk(LANES), blk(width),
                  _const_spec(cw.shape), _const_spec((1, xbc.shape[1])), _const_spec((1, LANES)),
                  _const_spec((1, LANES)), _const_spec((1, width)), _const_spec((1, width)),
                  _const_spec((cs, cs)), _const_spec((LANES, width))],
        out_specs=blk(width),
        out_shape=jax.ShapeDtypeStruct((t, width), BF16),
        scratch_shapes=[pltpu.VMEM((cs + 8, xbc.shape[1]), F32),
                        pltpu.VMEM((SSD_GROUPS, SSD_STATE, width // SSD_GROUPS), F32),
                        pltpu.VMEM((cs, width), F32)],
        compiler_params=_params(("parallel", "arbitrary")),
        name="ssd",
    )(xbc, dt, z, cw, cb[None, :], pad_h(dtb), pad_h(a_neg),
      jnp.repeat(dsk.astype(F32), SSD_HEAD_DIM)[None, :], nw[None, :], tri, expand)


def _rel_bias_tile(tab_ref, head, n_heads, delta, tk, tq):
    key = lax.broadcasted_iota(jnp.int32, (tk, tq), 0)
    qry = lax.broadcasted_iota(jnp.int32, (tk, tq), 1)
    n = jnp.maximum(delta + qry - key, 0)
    max_exact = REL_BUCKETS // 2
    large = max_exact + (jnp.log(jnp.maximum(n, 1).astype(F32) / max_exact)
                         / math.log(REL_MAX_DIST / max_exact) * (REL_BUCKETS - max_exact)).astype(jnp.int32)
    bucket = jnp.where(n < max_exact, n, jnp.minimum(large, REL_BUCKETS - 1))
    far = tab_ref[(REL_BUCKETS - 1) * n_heads + head]
    bias = jnp.zeros((tk, tq), F32)
    for j in range(REL_BUCKETS - 1):
        bias = jnp.where(bucket == j, tab_ref[j * n_heads + head] - far, bias)
    return bias


def _attn_body(tab_ref, lam_ref, qt_ref, k_ref, vt_ref, g_ref, sw_ref, o_ref, bias_ref, *, n_heads, seq):
    tq = tk = ATTN_BLOCK
    nq = seq // tq
    head = pl.program_id(0)

    @pl.when(pl.program_id(1) == 0)
    def _():
        for i, delta in enumerate((0, tq)):
            b = _rel_bias_tile(tab_ref, head, n_heads, delta, tk, tq)
            bias_ref[i] = jnp.concatenate([b, b], axis=1)

    lam = lam_ref[0]
    out_scale = lam_ref[1]
    feat = lax.broadcasted_iota(jnp.int32, (DA_V_DIM, tq), 0)
    first_map = feat < DA_QK_DIM
    key = lax.broadcasted_iota(jnp.int32, (tk, 2 * tq), 0)
    qry = lax.broadcasted_iota(jnp.int32, (tk, 2 * tq), 1)
    causal = jnp.where(qry >= tq, qry - tq, qry) >= key

    def scores(base, qst, k0, bias, masked):
        s = _dot(k_ref[pl.ds(base + k0, tk), :], qst)
        if bias is not None:
            s = s + bias
        if masked:
            s = jnp.where(causal, s, NEG_INF)
        return s

    def update(base, s, k0, carry):
        m, l, acc = carry
        m_new = jnp.maximum(m, jnp.max(s, axis=0, keepdims=True))
        alpha = jnp.exp(m - m_new)
        p = jnp.exp(s - m_new)
        l = alpha * l + jnp.sum(p, axis=0, keepdims=True)
        acc = alpha * acc + _dot(vt_ref[:, pl.ds(base + k0, tk)], p.astype(BF16))
        return m_new, l, acc

    def q_block(qi, q0):
        bases = [bi * seq for bi in range(ATTN_BATCH)]
        qsts = []
        for base in bases:
            qb = qt_ref[:, pl.ds(base + q0, tq)]
            zero = jnp.zeros_like(qb)
            qsts.append(jnp.concatenate([jnp.where(first_map, qb, zero), jnp.where(first_map, zero, qb)], axis=1))
        init = (jnp.full((1, 2 * tq), NEG_INF, F32), jnp.zeros((1, 2 * tq), F32),
                jnp.zeros((DA_V_DIM, 2 * tq), F32))

        def all_chains(k0, carries, bias, masked):
            ss = [scores(base, qst, k0, bias, masked) for base, qst in zip(bases, qsts)]
            return tuple(update(base, s, k0, cr) for base, s, cr in zip(bases, ss, carries))

        carries = (init,) * ATTN_BATCH
        if qi is not None:
            carries = lax.fori_loop(
                0, qi - 1, lambda ki, cr: all_chains(pl.multiple_of(ki * tk, tk), cr, None, False), carries)
            carries = all_chains(pl.multiple_of(q0 - tk, tk), carries, bias_ref[1], False)
        carries = all_chains(q0, carries, bias_ref[0], True)
        for base, (_, l, acc) in zip(bases, carries):
            o_t = acc[:, :tq] / l[:, :tq] - lam * (acc[:, tq:] / l[:, tq:])
            o = o_t.T
            o = o * _rms_scale(o) * sw_ref[...] * out_scale
            o_ref[pl.ds(base + q0, tq), :] = (o * _silu(g_ref[pl.ds(base + q0, tq), :])).astype(BF16)

    q_block(None, 0)

    def later_block(qi, _):
        q_block(qi, pl.multiple_of(qi * tq, tq))
        return 0

    lax.fori_loop(1, nq, later_block, 0)


def _diff_attn(qt, k, vt, g, table, lam_pair, sw, batch, seq):
    t, width = k.shape
    n_heads = width // DA_V_DIM
    blk = pl.BlockSpec((ATTN_BATCH * seq, DA_V_DIM), lambda h, b: (b, h))
    blk_t = pl.BlockSpec((DA_V_DIM, ATTN_BATCH * seq), lambda h, b: (h, b))
    smem = pl.BlockSpec(memory_space=pltpu.SMEM)
    return pl.pallas_call(
        functools.partial(_attn_body, n_heads=n_heads, seq=seq),
        grid=(n_heads, batch // ATTN_BATCH),
        in_specs=[smem, smem, blk_t, blk, blk_t, blk, _const_spec((1, DA_V_DIM))],
        out_specs=blk,
        out_shape=jax.ShapeDtypeStruct((t, width), BF16),
        scratch_shapes=[pltpu.VMEM((2, ATTN_BLOCK, 2 * ATTN_BLOCK), F32)],
        compiler_params=_params(("arbitrary", "arbitrary")),
        name="diff_attn",
    )(table.reshape(-1), lam_pair, qt, k, vt, g, sw[None, :])


def _hgrn_body(q_ref, k_ref, lf_ref, v_ref, g_ref, nw_ref, tri_ref, lvl_ref, y_ref, st_ref, b_ref):
    @pl.when(pl.program_id(2) == 0)
    def _():
        st_ref[...] = jnp.zeros(st_ref.shape, F32)

    c = q_ref.shape[0]
    dk = HG_HEAD_DIM
    heads = [slice(hi * dk, (hi + 1) * dk) for hi in range(HG_HEADS_PER_STEP)]
    lf_hi, lf_lo = _split_bf16(lf_ref[...], 2)
    b_ref[...] = _dot(tri_ref[...], jnp.concatenate([lf_hi, lf_lo], axis=0))

    outs = []
    for hi, hs in enumerate(heads):
        outs.append(_dot_nt((q_ref[:, hs] * jnp.exp(b_ref[:, hs])).astype(BF16), st_ref[hi].astype(BF16)))
    for hi, hs in enumerate(heads):
        b_last = b_ref[c - 1:c, hs]
        kd = (k_ref[:, hs] * jnp.exp(b_last - b_ref[:, hs])).astype(BF16)
        st_ref[hi] = st_ref[hi] * jnp.exp(b_last) + _dot_tn(v_ref[:, hs], kd)

    row = lax.broadcasted_iota(jnp.int32, (c, dk), 0)
    lvl = lvl_ref[...]
    atts = [jnp.where(lvl == 0, jnp.sum(q_ref[:, hs] * k_ref[:, hs], axis=-1, keepdims=True), 0.0)
            for hs in heads]
    m = c // 2
    while m >= 1:
        is_query = (row & m) != 0
        zs = []
        for hs in heads:
            b = b_ref[:, hs]
            ref_rows = jnp.concatenate(
                [jnp.broadcast_to(b[u + m - 1:u + m, :], (2 * m, dk)) for u in range(0, c, 2 * m)], axis=0)
            d = b - ref_rows
            zs.append((jnp.where(is_query, q_ref[:, hs], k_ref[:, hs])
                       * jnp.exp(jnp.where(is_query, d, -d))).astype(BF16))
        atts = [jnp.where(lvl == m, _dot_nt(z, z), att) for z, att in zip(zs, atts)]
        m //= 2

    for hi, hs in enumerate(heads):
        o = outs[hi] + _dot(atts[hi].astype(BF16), v_ref[:, hs])
        o = o * _rms_scale(o) * nw_ref[...]
        y_ref[:, hs] = (o * _silu(g_ref[:, hs])).astype(BF16)


def _hgrn(q, k, lf, v, g, nw, batch, seq):
    t, width = q.shape
    c = HG_CHUNK
    nc = seq // c
    bw = HG_HEADS_PER_STEP * HG_HEAD_DIM
    idx = jnp.arange(c)
    tri = (idx[:, None] >= idx[None, :]).astype(BF16)
    tri2 = jnp.concatenate([tri, tri], axis=1)
    diff = idx[:, None] ^ idx[None, :]
    top_bit = jnp.left_shift(1, jnp.maximum(31 - lax.clz(diff), 0))
    lvl = jnp.where(idx[:, None] > idx[None, :], top_bit, jnp.where(diff == 0, 0, -1)).astype(jnp.int32)
    blk = pl.BlockSpec((c, bw), lambda b, hg, ci: (b * nc + ci, hg))
    return pl.pallas_call(
        _hgrn_body,
        grid=(batch, width // bw, nc),
        in_specs=[blk, blk, blk, blk, blk, _const_spec((1, HG_HEAD_DIM)), _const_spec((c, 2 * c)),
                  _const_spec((c, c))],
        out_specs=blk,
        out_shape=jax.ShapeDtypeStruct((t, width), BF16),
        scratch_shapes=[pltpu.VMEM((HG_HEADS_PER_STEP, HG_HEAD_DIM, HG_HEAD_DIM), F32),
                        pltpu.VMEM((c, bw), F32)],
        compiler_params=_params(("parallel", "parallel", "arbitrary")),
        name="hgrn",
    )(q, k, lf, v, g, nw[None, :], tri2, lvl)


def kernel(x, norm_w, final_norm_w, rel_bias, even_w_in, even_w_out, conv_w, conv_b, dt_bias, A_log, D_skip,
           ssd_norm_w, lambda_q1, lambda_k1, lambda_q2, lambda_k2, subln_w, odd_w_in, odd_w_out,
           hgrn_lower_bounds, hgrn_norm_w):
    batch, seq, d = x.shape
    depth = norm_w.shape[0]
    tm = PROJ_ROWS
    h = x.reshape(batch * seq, d)

    heads = A_log.shape[1]
    ssd_w = heads * SSD_HEAD_DIM
    conv_ch = conv_w.shape[2]
    da_w = rel_bias.shape[1] * DA_V_DIM
    offs = [0]
    for n in (ssd_w, conv_ch, heads, da_w, da_w, da_w, da_w):
        offs.append(offs[-1] + n)

    lb_all = jax.nn.softmax(hgrn_lower_bounds.astype(F32), axis=0)
    lb_all = jnp.cumsum(lb_all, axis=0) - lb_all[0]

    for layer in range(depth):
        nw = norm_w[layer][None, :]
        last = layer == depth - 1
        if layer % 2 == 0:
            e = layer // 2
            w = even_w_in[e]
            seg = lambda i: w[:, offs[i]:offs[i + 1]]
            w_perm = jnp.concatenate(
                [seg(0), seg(1), seg(4), seg(6), jnp.pad(seg(2), ((0, 0), (0, LANES - heads)))],
                axis=1).astype(BF16)
            w_t = jnp.concatenate([seg(3) * (DA_QK_DIM ** -0.5), seg(5)], axis=1).T.astype(BF16)
            z, xbc, k, g, dt, q_t, v_t = _proj_even(h, nw, w_perm, w_t, tm)
            y_a = _ssd(xbc, dt, z, conv_w[e], conv_b[e], dt_bias[e], -jnp.exp(A_log[e].astype(F32)),
                       D_skip[e], ssd_norm_w[e], batch, seq)
            lam_init = 0.8 - 0.6 * math.exp(-0.3 * layer)
            lam = (jnp.exp(jnp.sum(lambda_q1[e].astype(F32) * lambda_k1[e].astype(F32)))
                   - jnp.exp(jnp.sum(lambda_q2[e].astype(F32) * lambda_k2[e].astype(F32))) + lam_init)
            lam_pair = jnp.stack([lam, jnp.asarray(1.0 - lam_init, F32)])
            y_b = _diff_attn(q_t, k, v_t, g, rel_bias.astype(F32), lam_pair, subln_w[e], batch, seq)
            w_out = even_w_out[e].astype(BF16)
            h = _out_proj(h, [y_a, y_b], [w_out[:ssd_w], w_out[ssd_w:]], tm,
                          final_norm_w[None, :] if last else None)
        else:
            o = layer // 2
            q, k, lf, v, g = _proj_odd(h, nw, odd_w_in[o].astype(BF16), lb_all[layer][None, :], tm // 2)
            y = _hgrn(q, k, lf, v, g, hgrn_norm_w[o], batch, seq)
            h = _out_proj(h, [y], [odd_w_out[o].astype(BF16)], tm, final_norm_w[None, :] if last else None)
    return h.reshape(batch, seq, d)
```

```python
import functools
import math

import jax
import jax.numpy as jnp
from jax import lax
from jax.experimental import pallas as pl
from jax.experimental.pallas import tpu as pltpu

F32 = jnp.float32
BF16 = jnp.bfloat16
EPS = 1e-6
NEG_INF = float("-inf")
LOG2E = math.log2(math.e)

LANES = 128
SSD_HEAD_DIM = 64
SSD_GROUPS = 4
SSD_STATE = 128
SSD_CONV = 4
SSD_CHUNK = 128
DA_QK_DIM = 64
DA_V_DIM = 128
REL_BUCKETS = 32
REL_MAX_DIST = 128
HG_HEAD_DIM = 128
HG_CHUNK = 128
HG_HEADS_PER_STEP = 16
ATTN_BLOCK = 256
ATTN_BATCH = 4
PROJ_ROWS = 512
VMEM_LIMIT = 56 * 1024 * 1024


def _dot(a, b):
    return jnp.dot(a, b, preferred_element_type=F32)


def _dot_nt(a, b):
    return lax.dot_general(a, b, (((1,), (1,)), ((), ())), preferred_element_type=F32)


def _dot_tn(a, b):
    return lax.dot_general(a, b, (((0,), (0,)), ((), ())), preferred_element_type=F32)


def _split_bf16(x, parts):
    out = []
    r = x
    for _ in range(parts):
        p = r.astype(BF16)
        out.append(p)
        r = r - p.astype(F32)
    return out


def _dot_exact_rhs(x, m_bf16, parts):
    return sum(_dot(p, m_bf16) for p in _split_bf16(x, parts))


def _dot_exact_lhs(m_bf16, x, parts):
    return sum(_dot(m_bf16, p) for p in _split_bf16(x, parts))


def _sigmoid(x):
    return 1.0 / (1.0 + jnp.exp(-x))


def _silu(x):
    return x * _sigmoid(x)


def _rms_scale(x):
    return lax.rsqrt(jnp.mean(x * x, axis=-1, keepdims=True) + EPS)


def _const_spec(shape):
    nd = len(shape)
    return pl.BlockSpec(shape, lambda *_: (0,) * nd, pipeline_mode=pl.Buffered(1))


def _params(sem):
    return pltpu.CompilerParams(dimension_semantics=sem, vmem_limit_bytes=VMEM_LIMIT)


def _proj_even_body(h_ref, nw_ref, w_ref, wt_ref, z_ref, xbc_ref, k_ref, g_ref, dt_ref, qt_ref, vt_ref):
    x = h_ref[...]
    u = (x * _rms_scale(x) * nw_ref[...]).astype(BF16)
    col = 0
    for ref in (z_ref, xbc_ref, k_ref, g_ref, dt_ref):
        n = ref.shape[1]
        step = min(n, 512)
        for j in range(0, n, step):
            ref[:, j:j + step] = _dot(u, w_ref[:, col + j:col + j + step]).astype(ref.dtype)
        col += n
    row = 0
    for ref in (qt_ref, vt_ref):
        n = ref.shape[0]
        for j in range(0, n, 256):
            ref[j:j + 256, :] = _dot_nt(wt_ref[row + j:row + j + 256, :], u).astype(ref.dtype)
        row += n


def _proj_even(h, nw, w, wt, tm):
    t, d = h.shape
    widths = (d, 2 * d, d, d, LANES)
    dtypes = (F32, F32, BF16, F32, F32)
    row = lambda n: pl.BlockSpec((tm, n), lambda i: (i, 0))
    col = pl.BlockSpec((d, tm), lambda i: (0, i))
    return pl.pallas_call(
        _proj_even_body,
        grid=(t // tm,),
        in_specs=[row(d), _const_spec((1, d)), _const_spec(w.shape), _const_spec(wt.shape)],
        out_specs=[row(n) for n in widths] + [col, col],
        out_shape=[jax.ShapeDtypeStruct((t, n), dt) for n, dt in zip(widths, dtypes)]
        + [jax.ShapeDtypeStruct((d, t), BF16)] * 2,
        compiler_params=_params(("parallel",)),
        name="proj_even",
    )(h, nw, w, wt)


def _proj_odd_body(h_ref, nw_ref, w_ref, lb_ref, q_ref, k_ref, lf_ref, v_ref, g_ref):
    x = h_ref[...]
    u = (x * _rms_scale(x) * nw_ref[...]).astype(BF16)
    n = q_ref.shape[1]
    step = 512
    for j in range(0, n, step):
        cs = slice(j, j + step)
        q_ref[:, cs] = _silu(_dot(u, w_ref[:, j:j + step]))
        lb = lb_ref[:, cs]
        f = lb + (1.0 - lb) * _sigmoid(_dot(u, w_ref[:, n + j:n + j + step]))
        lf_ref[:, cs] = jnp.log(f) * LOG2E
        k_ref[:, cs] = 1.0 - f
        v_ref[:, cs] = _dot(u, w_ref[:, 2 * n + j:2 * n + j + step]).astype(BF16)
        g_ref[:, cs] = _dot(u, w_ref[:, 3 * n + j:3 * n + j + step])


def _proj_odd(h, nw, w, lb, tm):
    t, d = h.shape
    n = w.shape[1] // 4
    dtypes = (F32, F32, F32, BF16, F32)
    row = lambda m: pl.BlockSpec((tm, m), lambda i: (i, 0))
    return pl.pallas_call(
        _proj_odd_body,
        grid=(t // tm,),
        in_specs=[row(d), _const_spec((1, d)), _const_spec(w.shape), _const_spec((1, n))],
        out_specs=[row(n) for _ in dtypes],
        out_shape=[jax.ShapeDtypeStruct((t, n), dt) for dt in dtypes],
        compiler_params=_params(("parallel",)),
        name="proj_odd",
    )(h, nw, w, lb)


def _out_proj_body(*refs, n_act, final):
    h_ref = refs[0]
    acts = refs[1:1 + n_act]
    ws = refs[1 + n_act:1 + 2 * n_act]
    rest = refs[1 + 2 * n_act:]
    acc = h_ref[...]
    for a_ref, w_ref in zip(acts, ws):
        acc = acc + _dot(a_ref[...], w_ref[...])
    if final:
        fw_ref, o_ref = rest
        acc = acc * _rms_scale(acc) * fw_ref[...]
    else:
        (o_ref,) = rest
    o_ref[...] = acc


def _out_proj(h, acts, ws, tm, final_w=None):
    t, d = h.shape
    final = final_w is not None
    row = lambda n: pl.BlockSpec((tm, n), lambda i: (i, 0))
    in_specs = [row(d)] + [row(a.shape[1]) for a in acts] + [_const_spec(w.shape) for w in ws]
    args = [h, *acts, *ws]
    if final:
        in_specs.append(_const_spec((1, d)))
        args.append(final_w)
    return pl.pallas_call(
        functools.partial(_out_proj_body, n_act=len(acts), final=final),
        grid=(t // tm,),
        in_specs=in_specs,
        out_specs=row(d),
        out_shape=jax.ShapeDtypeStruct((t, d), F32),
        compiler_params=_params(("parallel",)),
        name="out_proj",
    )(*args)


def _softplus(x):
    return jnp.maximum(x, 0.0) + jnp.log(1.0 + jnp.exp(-jnp.abs(x)))


def _ssd_body(xbc_ref, dt_ref, z_ref, cw_ref, cb_ref, dtb_ref, a_ref, dsk_ref, nw_ref, tri_ref,
              exp_ref, y_ref, ext_ref, st_ref, yacc_ref, *, width):
    cs = SSD_CHUNK
    gw = width // SSD_GROUPS
    hpg = gw // SSD_HEAD_DIM
    pad = 8
    c = pl.program_id(1)

    @pl.when(c == 0)
    def _():
        ext_ref[0:pad, :] = jnp.zeros((pad, ext_ref.shape[1]), F32)
        st_ref[...] = jnp.zeros(st_ref.shape, F32)

    xin = xbc_ref[...]
    ext_ref[pad:pad + cs, :] = xin
    conv = cb_ref[...] + cw_ref[SSD_CONV - 1:SSD_CONV, :] * xin
    for j in range(1, SSD_CONV):
        conv = conv + cw_ref[SSD_CONV - 1 - j:SSD_CONV - j, :] * ext_ref[pad - j:pad - j + cs, :]
    ext_ref[0:pad, :] = xin[cs - pad:cs, :]
    act = _silu(conv)

    dt = _softplus(dt_ref[...] + dtb_ref[...])
    a = dt * a_ref[...]
    a_cs = _dot_exact_lhs(tri_ref[...], a, 3)
    a_cs_t = a_cs.T
    dt_t = dt.T
    a_last = a_cs[cs - 1:cs, :]
    per_head = jnp.concatenate(
        [dt * jnp.exp2(a_last - a_cs), jnp.exp2(a_cs), jnp.broadcast_to(jnp.exp2(a_last), (8, LANES))], axis=0)
    per_col = _dot_exact_rhs(per_head, exp_ref[...], 2)
    w_state = per_col[0:cs]
    e_acs = per_col[cs:2 * cs]
    e_last = per_col[2 * cs:2 * cs + 1]

    row = lax.broadcasted_iota(jnp.int32, (cs, cs), 0)
    colm = lax.broadcasted_iota(jnp.int32, (cs, cs), 1)
    tril = row >= colm
    lane = lax.broadcasted_iota(jnp.int32, (cs, LANES), 1)
    low_half = lane < SSD_HEAD_DIM

    for g in range(SSD_GROUPS):
        gs = slice(g * gw, (g + 1) * gw)
        xg = act[:, gs]
        bg = act[:, width + g * SSD_STATE:width + (g + 1) * SSD_STATE].astype(BF16)
        cg = act[:, width + (SSD_GROUPS + g) * SSD_STATE:width + (SSD_GROUPS + g + 1) * SSD_STATE].astype(BF16)
        cb = _dot_nt(cg, bg)
        s_prev = st_ref[g]
        y_off = _dot(cg, s_prev.astype(BF16)) * e_acs[:, gs]
        xd = (xg * w_state[:, gs]).astype(BF16)
        st_ref[g] = s_prev * e_last[:, gs] + _dot_tn(bg, xd)
        for p in range(gw // LANES):
            xp = xg[:, p * LANES:(p + 1) * LANES]
            halves = (jnp.where(low_half, xp, 0.0).astype(BF16), jnp.where(low_half, 0.0, xp).astype(BF16))
            acc = y_off[:, p * LANES:(p + 1) * LANES] + xp * dsk_ref[:, g * gw + p * LANES:g * gw + (p + 1) * LANES]
            for r in range(LANES // SSD_HEAD_DIM):
                hd = g * hpg + p * (LANES // SSD_HEAD_DIM) + r
                decay = jnp.exp2(jnp.where(tril, a_cs[:, hd:hd + 1] - a_cs_t[hd:hd + 1, :], NEG_INF))
                scores = (cb * decay * dt_t[hd:hd + 1, :]).astype(BF16)
                acc = acc + _dot(scores, halves[r])
            yacc_ref[:, g * gw + p * LANES:g * gw + (p + 1) * LANES] = acc

    yz = yacc_ref[...] * _silu(z_ref[...])
    y_ref[...] = (yz * _rms_scale(yz) * nw_ref[...]).astype(BF16)


def _ssd(xbc, dt, z, cw, cb, dtb, a_neg, dsk, nw, batch, seq):
    t, width = z.shape
    cs = SSD_CHUNK
    nc = seq // cs
    heads = width // SSD_HEAD_DIM
    tri = (jnp.arange(cs)[:, None] >= jnp.arange(cs)[None, :]).astype(BF16)
    expand = (jnp.arange(LANES)[:, None] == (jnp.arange(width)[None, :] // SSD_HEAD_DIM)).astype(BF16)
    pad_h = lambda v: jnp.pad(v.astype(F32), (0, LANES - heads))[None, :]
    blk = lambda n: pl.BlockSpec((cs, n), lambda b, c: (b * nc + c, 0))
    return pl.pallas_call(
        functools.partial(_ssd_body, width=width),
        grid=(batch, nc),
        in_specs=[blk(xbc.shape[1]), blk(LANES), blk(width),
                  _const_spec(cw.shape), _const_spec((1, xbc.shape[1])), _const_spec((1, LANES)),
                  _const_spec((1, LANES)), _const_spec((1, width)), _const_spec((1, width)),
                  _const_spec((cs, cs)), _const_spec((LANES, width))],
        out_specs=blk(width),
        out_shape=jax.ShapeDtypeStruct((t, width), BF16),
        scratch_shapes=[pltpu.VMEM((cs + 8, xbc.shape[1]), F32),
                        pltpu.VMEM((SSD_GROUPS, SSD_STATE, width // SSD_GROUPS), F32),
                        pltpu.VMEM((cs, width), F32)],
        compiler_params=_params(("parallel", "arbitrary")),
        name="ssd",
    )(xbc, dt, z, cw, cb[None, :], pad_h(dtb), pad_h(a_neg),
      jnp.repeat(dsk.astype(F32), SSD_HEAD_DIM)[None, :], nw[None, :], tri, expand)


def _rel_bias_tile(tab_ref, head, n_heads, delta, tk, tq):
    key = lax.broadcasted_iota(jnp.int32, (tk, tq), 0)
    qry = lax.broadcasted_iota(jnp.int32, (tk, tq), 1)
    n = jnp.maximum(delta + qry - key, 0)
    max_exact = REL_BUCKETS // 2
    large = max_exact + (jnp.log(jnp.maximum(n, 1).astype(F32) / max_exact)
                         / math.log(REL_MAX_DIST / max_exact) * (REL_BUCKETS - max_exact)).astype(jnp.int32)
    bucket = jnp.where(n < max_exact, n, jnp.minimum(large, REL_BUCKETS - 1))
    far = tab_ref[(REL_BUCKETS - 1) * n_heads + head]
    bias = jnp.zeros((tk, tq), F32)
    for j in range(REL_BUCKETS - 1):
        bias = jnp.where(bucket == j, tab_ref[j * n_heads + head] - far, bias)
    return bias * LOG2E


def _attn_body(tab_ref, lam_ref, qt_ref, k_ref, vt_ref, g_ref, sw_ref, o_ref, bias_ref, *, n_heads, seq):
    tq = tk = ATTN_BLOCK
    nq = seq // tq
    head = pl.program_id(0)

    @pl.when(pl.program_id(1) == 0)
    def _():
        for i, delta in enumerate((0, tq)):
            b = _rel_bias_tile(tab_ref, head, n_heads, delta, tk, tq)
            bias_ref[i] = jnp.concatenate([b, b], axis=1)

    lam = lam_ref[0]
    out_scale = lam_ref[1]
    feat = lax.broadcasted_iota(jnp.int32, (DA_V_DIM, tq), 0)
    first_map = feat < DA_QK_DIM
    key = lax.broadcasted_iota(jnp.int32, (tk, 2 * tq), 0)
    qry = lax.broadcasted_iota(jnp.int32, (tk, 2 * tq), 1)
    causal = jnp.where(qry >= tq, qry - tq, qry) >= key

    def scores(base, qst, k0, bias, masked):
        s = _dot(k_ref[pl.ds(base + k0, tk), :], qst)
        if bias is not None:
            s = s + bias
        if masked:
            s = jnp.where(causal, s, NEG_INF)
        return s

    def update(base, s, k0, carry):
        m, l, acc = carry
        m_new = jnp.maximum(m, jnp.max(s, axis=0, keepdims=True))
        alpha = jnp.exp2(m - m_new)
        p = jnp.exp2(s - m_new)
        l = alpha * l + jnp.sum(p, axis=0, keepdims=True)
        acc = alpha * acc + _dot(vt_ref[:, pl.ds(base + k0, tk)], p.astype(BF16))
        return m_new, l, acc

    def q_block(qi, q0):
        bases = [bi * seq for bi in range(ATTN_BATCH)]
        qsts = []
        for base in bases:
            qb = qt_ref[:, pl.ds(base + q0, tq)]
            zero = jnp.zeros_like(qb)
            qsts.append(jnp.concatenate([jnp.where(first_map, qb, zero), jnp.where(first_map, zero, qb)], axis=1))
        init = (jnp.full((1, 2 * tq), NEG_INF, F32), jnp.zeros((1, 2 * tq), F32),
                jnp.zeros((DA_V_DIM, 2 * tq), F32))

        def group(tiles, carries):
            ss = [[scores(base, qst, *tile) for base, qst in zip(bases, qsts)] for tile in tiles]
            for (k0, _, _), s_tile in zip(tiles, ss):
                carries = tuple(update(base, s, k0, cr) for base, s, cr in zip(bases, s_tile, carries))
            return carries

        far = lambda ki: (pl.multiple_of(ki * tk, tk), None, False)
        carries = (init,) * ATTN_BATCH
        if qi is None:
            carries = group([(q0, bias_ref[0], True)], carries)
        else:
            n_far = qi - 1
            carries = lax.fori_loop(
                0, n_far // 2, lambda kk, cr: group([far(2 * kk), far(2 * kk + 1)], cr), carries)
            carries = lax.cond(n_far % 2 == 1, lambda cr: group([far(n_far - 1)], cr), lambda cr: cr, carries)
            carries = group([(pl.multiple_of(q0 - tk, tk), bias_ref[1], False), (q0, bias_ref[0], True)], carries)
        for base, (_, l, acc) in zip(bases, carries):
            o_t = acc[:, :tq] / l[:, :tq] - lam * (acc[:, tq:] / l[:, tq:])
            o = o_t.T
            o = o * _rms_scale(o) * sw_ref[...] * out_scale
            o_ref[pl.ds(base + q0, tq), :] = (o * _silu(g_ref[pl.ds(base + q0, tq), :])).astype(BF16)

    q_block(None, 0)

    def later_block(qi, _):
        q_block(qi, pl.multiple_of(qi * tq, tq))
        return 0

    lax.fori_loop(1, nq, later_block, 0)


def _diff_attn(qt, k, vt, g, table, lam_pair, sw, batch, seq):
    t, width = k.shape
    n_heads = width // DA_V_DIM
    blk = pl.BlockSpec((ATTN_BATCH * seq, DA_V_DIM), lambda h, b: (b, h))
    blk_t = pl.BlockSpec((DA_V_DIM, ATTN_BATCH * seq), lambda h, b: (h, b))
    smem = pl.BlockSpec(memory_space=pltpu.SMEM)
    return pl.pallas_call(
        functools.partial(_attn_body, n_heads=n_heads, seq=seq),
        grid=(n_heads, batch // ATTN_BATCH),
        in_specs=[smem, smem, blk_t, blk, blk_t, blk, _const_spec((1, DA_V_DIM))],
        out_specs=blk,
        out_shape=jax.ShapeDtypeStruct((t, width), BF16),
        scratch_shapes=[pltpu.VMEM((2, ATTN_BLOCK, 2 * ATTN_BLOCK), F32)],
        compiler_params=_params(("arbitrary", "arbitrary")),
        name="diff_attn",
    )(table.reshape(-1), lam_pair, qt, k, vt, g, sw[None, :])


def _hgrn_body(q_ref, k_ref, lf_ref, v_ref, g_ref, nw_ref, tri_ref, lvl_ref, y_ref, st_ref, b_ref):
    @pl.when(pl.program_id(2) == 0)
    def _():
        st_ref[...] = jnp.zeros(st_ref.shape, F32)

    c = q_ref.shape[0]
    dk = HG_HEAD_DIM
    heads = [slice(hi * dk, (hi + 1) * dk) for hi in range(HG_HEADS_PER_STEP)]
    lf_hi, lf_lo = _split_bf16(lf_ref[...], 2)
    b_ref[...] = _dot(tri_ref[...], jnp.concatenate([lf_hi, lf_lo], axis=0))

    outs = []
    for hi, hs in enumerate(heads):
        outs.append(_dot_nt((q_ref[:, hs] * jnp.exp2(b_ref[:, hs])).astype(BF16), st_ref[hi].astype(BF16)))
    for hi, hs in enumerate(heads):
        b_last = b_ref[c - 1:c, hs]
        kd = (k_ref[:, hs] * jnp.exp2(b_last - b_ref[:, hs])).astype(BF16)
        st_ref[hi] = st_ref[hi] * jnp.exp2(b_last) + _dot_tn(v_ref[:, hs], kd)

    row = lax.broadcasted_iota(jnp.int32, (c, dk), 0)
    lvl = lvl_ref[...]
    atts = [jnp.where(lvl == 0, jnp.sum(q_ref[:, hs] * k_ref[:, hs], axis=-1, keepdims=True), 0.0)
            for hs in heads]
    m = c // 2
    while m >= 1:
        is_query = (row & m) != 0
        zs = []
        for hs in heads:
            b = b_ref[:, hs]
            ref_rows = jnp.concatenate(
                [jnp.broadcast_to(b[u + m - 1:u + m, :], (2 * m, dk)) for u in range(0, c, 2 * m)], axis=0)
            d = b - ref_rows
            zs.append((jnp.where(is_query, q_ref[:, hs], k_ref[:, hs])
                       * jnp.exp2(jnp.where(is_query, d, -d))).astype(BF16))
        atts = [jnp.where(lvl == m, _dot_nt(z, z), att) for z, att in zip(zs, atts)]
        m //= 2

    for hi, hs in enumerate(heads):
        o = outs[hi] + _dot(atts[hi].astype(BF16), v_ref[:, hs])
        o = o * _rms_scale(o) * nw_ref[...]
        y_ref[:, hs] = (o * _silu(g_ref[:, hs])).astype(BF16)


def _hgrn(q, k, lf, v, g, nw, batch, seq):
    t, width = q.shape
    c = HG_CHUNK
    nc = seq // c
    bw = HG_HEADS_PER_STEP * HG_HEAD_DIM
    idx = jnp.arange(c)
    tri = (idx[:, None] >= idx[None, :]).astype(BF16)
    tri2 = jnp.concatenate([tri, tri], axis=1)
    diff = idx[:, None] ^ idx[None, :]
    top_bit = jnp.left_shift(1, jnp.maximum(31 - lax.clz(diff), 0))
    lvl = jnp.where(idx[:, None] > idx[None, :], top_bit, jnp.where(diff == 0, 0, -1)).astype(jnp.int32)
    blk = pl.BlockSpec((c, bw), lambda b, hg, ci: (b * nc + ci, hg))
    return pl.pallas_call(
        _hgrn_body,
        grid=(batch, width // bw, nc),
        in_specs=[blk, blk, blk, blk, blk, _const_spec((1, HG_HEAD_DIM)), _const_spec((c, 2 * c)),
                  _const_spec((c, c))],
        out_specs=blk,
        out_shape=jax.ShapeDtypeStruct((t, width), BF16),
        scratch_shapes=[pltpu.VMEM((HG_HEADS_PER_STEP, HG_HEAD_DIM, HG_HEAD_DIM), F32),
                        pltpu.VMEM((c, bw), F32)],
        compiler_params=_params(("parallel", "parallel", "arbitrary")),
        name="hgrn",
    )(q, k, lf, v, g, nw[None, :], tri2, lvl)


def kernel(x, norm_w, final_norm_w, rel_bias, even_w_in, even_w_out, conv_w, conv_b, dt_bias, A_log, D_skip,
           ssd_norm_w, lambda_q1, lambda_k1, lambda_q2, lambda_k2, subln_w, odd_w_in, odd_w_out,
           hgrn_lower_bounds, hgrn_norm_w):
    batch, seq, d = x.shape
    depth = norm_w.shape[0]
    tm = PROJ_ROWS
    h = x.reshape(batch * seq, d)

    heads = A_log.shape[1]
    ssd_w = heads * SSD_HEAD_DIM
    conv_ch = conv_w.shape[2]
    da_w = rel_bias.shape[1] * DA_V_DIM
    offs = [0]
    for n in (ssd_w, conv_ch, heads, da_w, da_w, da_w, da_w):
        offs.append(offs[-1] + n)

    lb_all = jax.nn.softmax(hgrn_lower_bounds.astype(F32), axis=0)
    lb_all = jnp.cumsum(lb_all, axis=0) - lb_all[0]

    for layer in range(depth):
        nw = norm_w[layer][None, :]
        last = layer == depth - 1
        if layer % 2 == 0:
            e = layer // 2
            w = even_w_in[e]
            seg = lambda i: w[:, offs[i]:offs[i + 1]]
            w_perm = jnp.concatenate(
                [seg(0), seg(1), seg(4), seg(6), jnp.pad(seg(2), ((0, 0), (0, LANES - heads)))],
                axis=1).astype(BF16)
            w_t = jnp.concatenate([seg(3) * (DA_QK_DIM ** -0.5 * LOG2E), seg(5)], axis=1).T.astype(BF16)
            z, xbc, k, g, dt, q_t, v_t = _proj_even(h, nw, w_perm, w_t, tm)
            y_a = _ssd(xbc, dt, z, conv_w[e], conv_b[e], dt_bias[e], -jnp.exp(A_log[e].astype(F32)) * LOG2E,
                       D_skip[e], ssd_norm_w[e], batch, seq)
            lam_init = 0.8 - 0.6 * math.exp(-0.3 * layer)
            lam = (jnp.exp(jnp.sum(lambda_q1[e].astype(F32) * lambda_k1[e].astype(F32)))
                   - jnp.exp(jnp.sum(lambda_q2[e].astype(F32) * lambda_k2[e].astype(F32))) + lam_init)
            lam_pair = jnp.stack([lam, jnp.asarray(1.0 - lam_init, F32)])
            y_b = _diff_attn(q_t, k, v_t, g, rel_bias.astype(F32), lam_pair, subln_w[e], batch, seq)
            w_out = even_w_out[e].astype(BF16)
            h = _out_proj(h, [y_a, y_b], [w_out[:ssd_w], w_out[ssd_w:]], tm,
                          final_norm_w[None, :] if last else None)
        else:
            o = layer // 2
            q, k, lf, v, g = _proj_odd(h, nw, odd_w_in[o].astype(BF16), lb_all[layer][None, :], tm // 2)
            y = _hgrn(q, k, lf, v, g, hgrn_norm_w[o], batch, seq)
            h = _out_proj(h, [y], [odd_w_out[o].astype(BF16)], tm, final_norm_w[None, :] if last else None)
    return h.reshape(batch, seq, d)
```

```python
import functools
import math

import jax
import jax.numpy as jnp
from jax import lax
from jax.experimental import pallas as pl
from jax.experimental.pallas import tpu as pltpu

F32 = jnp.float32
BF16 = jnp.bfloat16
EPS = 1e-6
NEG_INF = float("-inf")
LOG2E = math.log2(math.e)

LANES = 128
SSD_HEAD_DIM = 64
SSD_GROUPS = 4
SSD_STATE = 128
SSD_CONV = 4
SSD_CHUNK = 128
DA_QK_DIM = 64
DA_V_DIM = 128
REL_BUCKETS = 32
REL_MAX_DIST = 128
HG_HEAD_DIM = 128
HG_CHUNK = 128
HG_HEADS_PER_STEP = 16
ATTN_BLOCK = 256
ATTN_BATCH = 4
ATTN_ONES_ROWS = 16
PROJ_ROWS = 512
VMEM_LIMIT = 56 * 1024 * 1024


def _dot(a, b):
    return jnp.dot(a, b, preferred_element_type=F32)


def _dot_nt(a, b):
    return lax.dot_general(a, b, (((1,), (1,)), ((), ())), preferred_element_type=F32)


def _dot_tn(a, b):
    return lax.dot_general(a, b, (((0,), (0,)), ((), ())), preferred_element_type=F32)


def _split_bf16(x, parts):
    out = []
    r = x
    for _ in range(parts):
        p = r.astype(BF16)
        out.append(p)
        r = r - p.astype(F32)
    return out


def _dot_exact_rhs(x, m_bf16, parts):
    return sum(_dot(p, m_bf16) for p in _split_bf16(x, parts))


def _dot_exact_lhs(m_bf16, x, parts):
    return sum(_dot(m_bf16, p) for p in _split_bf16(x, parts))


def _sigmoid(x):
    return 1.0 / (1.0 + jnp.exp(-x))


def _silu(x):
    return x * _sigmoid(x)


def _rms_scale(x):
    return lax.rsqrt(jnp.mean(x * x, axis=-1, keepdims=True) + EPS)


def _const_spec(shape):
    nd = len(shape)
    return pl.BlockSpec(shape, lambda *_: (0,) * nd, pipeline_mode=pl.Buffered(1))


def _params(sem, flags=None):
    return pltpu.CompilerParams(dimension_semantics=sem, vmem_limit_bytes=VMEM_LIMIT, flags=flags)


def _proj_even_body(h_ref, nw_ref, w_ref, wt_ref, z_ref, xbc_ref, k_ref, g_ref, dt_ref, qt_ref, vt_ref):
    x = h_ref[...]
    u = (x * _rms_scale(x) * nw_ref[...]).astype(BF16)
    col = 0
    for ref in (z_ref, xbc_ref, k_ref, g_ref, dt_ref):
        n = ref.shape[1]
        step = min(n, 512)
        for j in range(0, n, step):
            ref[:, j:j + step] = _dot(u, w_ref[:, col + j:col + j + step]).astype(ref.dtype)
        col += n
    row = 0
    for ref in (qt_ref, vt_ref):
        n = ref.shape[0]
        for j in range(0, n, 256):
            ref[j:j + 256, :] = _dot_nt(wt_ref[row + j:row + j + 256, :], u).astype(ref.dtype)
        row += n


def _proj_even(h, nw, w, wt, tm):
    t, d = h.shape
    widths = (d, 2 * d, d, d, LANES)
    dtypes = (F32, F32, BF16, F32, F32)
    row = lambda n: pl.BlockSpec((tm, n), lambda i: (i, 0))
    col = pl.BlockSpec((d, tm), lambda i: (0, i))
    return pl.pallas_call(
        _proj_even_body,
        grid=(t // tm,),
        in_specs=[row(d), _const_spec((1, d)), _const_spec(w.shape), _const_spec(wt.shape)],
        out_specs=[row(n) for n in widths] + [col, col],
        out_shape=[jax.ShapeDtypeStruct((t, n), dt) for n, dt in zip(widths, dtypes)]
        + [jax.ShapeDtypeStruct((d, t), BF16)] * 2,
        compiler_params=_params(("parallel",)),
        name="proj_even",
    )(h, nw, w, wt)


def _proj_odd_body(h_ref, nw_ref, w_ref, lb_ref, q_ref, k_ref, lf_ref, v_ref, g_ref):
    x = h_ref[...]
    u = (x * _rms_scale(x) * nw_ref[...]).astype(BF16)
    n = q_ref.shape[1]
    step = 512
    for j in range(0, n, step):
        cs = slice(j, j + step)
        q_ref[:, cs] = _silu(_dot(u, w_ref[:, j:j + step]))
        lb = lb_ref[:, cs]
        f = lb + (1.0 - lb) * _sigmoid(_dot(u, w_ref[:, n + j:n + j + step]))
        lf_ref[:, cs] = jnp.log(f) * LOG2E
        k_ref[:, cs] = 1.0 - f
        v_ref[:, cs] = _dot(u, w_ref[:, 2 * n + j:2 * n + j + step]).astype(BF16)
        g_ref[:, cs] = _dot(u, w_ref[:, 3 * n + j:3 * n + j + step])


def _proj_odd(h, nw, w, lb, tm):
    t, d = h.shape
    n = w.shape[1] // 4
    dtypes = (F32, F32, F32, BF16, F32)
    row = lambda m: pl.BlockSpec((tm, m), lambda i: (i, 0))
    return pl.pallas_call(
        _proj_odd_body,
        grid=(t // tm,),
        in_specs=[row(d), _const_spec((1, d)), _const_spec(w.shape), _const_spec((1, n))],
        out_specs=[row(n) for _ in dtypes],
        out_shape=[jax.ShapeDtypeStruct((t, n), dt) for dt in dtypes],
        compiler_params=_params(("parallel",)),
        name="proj_odd",
    )(h, nw, w, lb)


def _out_proj_body(*refs, n_act, final):
    h_ref = refs[0]
    acts = refs[1:1 + n_act]
    ws = refs[1 + n_act:1 + 2 * n_act]
    rest = refs[1 + 2 * n_act:]
    acc = h_ref[...]
    for a_ref, w_ref in zip(acts, ws):
        acc = acc + _dot(a_ref[...], w_ref[...])
    if final:
        fw_ref, o_ref = rest
        acc = acc * _rms_scale(acc) * fw_ref[...]
    else:
        (o_ref,) = rest
    o_ref[...] = acc


def _out_proj(h, acts, ws, tm, final_w=None):
    t, d = h.shape
    final = final_w is not None
    row = lambda n: pl.BlockSpec((tm, n), lambda i: (i, 0))
    in_specs = [row(d)] + [row(a.shape[1]) for a in acts] + [_const_spec(w.shape) for w in ws]
    args = [h, *acts, *ws]
    if final:
        in_specs.append(_const_spec((1, d)))
        args.append(final_w)
    return pl.pallas_call(
        functools.partial(_out_proj_body, n_act=len(acts), final=final),
        grid=(t // tm,),
        in_specs=in_specs,
        out_specs=row(d),
        out_shape=jax.ShapeDtypeStruct((t, d), F32),
        compiler_params=_params(("parallel",)),
        name="out_proj",
    )(*args)


def _softplus(x):
    return jnp.maximum(x, 0.0) + jnp.log(1.0 + jnp.exp(-jnp.abs(x)))


def _ssd_body(xbc_ref, dt_ref, z_ref, cw_ref, cb_ref, dtb_ref, a_ref, dsk_ref, nw_ref, tri_ref,
              exp_ref, y_ref, ext_ref, st_ref, yacc_ref, *, width):
    cs = SSD_CHUNK
    gw = width // SSD_GROUPS
    hpg = gw // SSD_HEAD_DIM
    pad = 8
    c = pl.program_id(1)

    @pl.when(c == 0)
    def _():
        ext_ref[0:pad, :] = jnp.zeros((pad, ext_ref.shape[1]), F32)
        st_ref[...] = jnp.zeros(st_ref.shape, F32)

    xin = xbc_ref[...]
    ext_ref[pad:pad + cs, :] = xin
    conv = cb_ref[...] + cw_ref[SSD_CONV - 1:SSD_CONV, :] * xin
    for j in range(1, SSD_CONV):
        conv = conv + cw_ref[SSD_CONV - 1 - j:SSD_CONV - j, :] * ext_ref[pad - j:pad - j + cs, :]
    ext_ref[0:pad, :] = xin[cs - pad:cs, :]
    act = _silu(conv)

    dt = _softplus(dt_ref[...] + dtb_ref[...])
    a = dt * a_ref[...]
    a_cs = _dot_exact_lhs(tri_ref[...], a, 3)
    a_cs_t = a_cs.T
    dt_t = dt.T
    a_last = a_cs[cs - 1:cs, :]
    per_head = jnp.concatenate(
        [dt * jnp.exp2(a_last - a_cs), jnp.exp2(a_cs), jnp.broadcast_to(jnp.exp2(a_last), (8, LANES))], axis=0)
    per_col = _dot_exact_rhs(per_head, exp_ref[...], 2)
    w_state = per_col[0:cs]
    e_acs = per_col[cs:2 * cs]
    e_last = per_col[2 * cs:2 * cs + 1]

    row = lax.broadcasted_iota(jnp.int32, (cs, cs), 0)
    colm = lax.broadcasted_iota(jnp.int32, (cs, cs), 1)
    tril = row >= colm
    lane = lax.broadcasted_iota(jnp.int32, (cs, LANES), 1)
    low_half = lane < SSD_HEAD_DIM

    for g in range(SSD_GROUPS):
        gs = slice(g * gw, (g + 1) * gw)
        xg = act[:, gs]
        bg = act[:, width + g * SSD_STATE:width + (g + 1) * SSD_STATE].astype(BF16)
        cg = act[:, width + (SSD_GROUPS + g) * SSD_STATE:width + (SSD_GROUPS + g + 1) * SSD_STATE].astype(BF16)
        cb = _dot_nt(cg, bg)
        s_prev = st_ref[g]
        y_off = _dot(cg, s_prev.astype(BF16)) * e_acs[:, gs]
        xd = (xg * w_state[:, gs]).astype(BF16)
        st_ref[g] = s_prev * e_last[:, gs] + _dot_tn(bg, xd)
        for p in range(gw // LANES):
            xp = xg[:, p * LANES:(p + 1) * LANES]
            halves = (jnp.where(low_half, xp, 0.0).astype(BF16), jnp.where(low_half, 0.0, xp).astype(BF16))
            acc = y_off[:, p * LANES:(p + 1) * LANES] + xp * dsk_ref[:, g * gw + p * LANES:g * gw + (p + 1) * LANES]
            for r in range(LANES // SSD_HEAD_DIM):
                hd = g * hpg + p * (LANES // SSD_HEAD_DIM) + r
                decay = jnp.exp2(jnp.where(tril, a_cs[:, hd:hd + 1] - a_cs_t[hd:hd + 1, :], NEG_INF))
                scores = (cb * decay * dt_t[hd:hd + 1, :]).astype(BF16)
                acc = acc + _dot(scores, halves[r])
            yacc_ref[:, g * gw + p * LANES:g * gw + (p + 1) * LANES] = acc

    yz = yacc_ref[...] * _silu(z_ref[...])
    y_ref[...] = (yz * _rms_scale(yz) * nw_ref[...]).astype(BF16)


def _ssd(xbc, dt, z, cw, cb, dtb, a_neg, dsk, nw, batch, seq):
    t, width = z.shape
    cs = SSD_CHUNK
    nc = seq // cs
    heads = width // SSD_HEAD_DIM
    tri = (jnp.arange(cs)[:, None] >= jnp.arange(cs)[None, :]).astype(BF16)
    expand = (jnp.arange(LANES)[:, None] == (jnp.arange(width)[None, :] // SSD_HEAD_DIM)).astype(BF16)
    pad_h = lambda v: jnp.pad(v.astype(F32), (0, LANES - heads))[None, :]
    blk = lambda n: pl.BlockSpec((cs, n), lambda b, c: (b * nc + c, 0))
    return pl.pallas_call(
        functools.partial(_ssd_body, width=width),
        grid=(batch, nc),
        in_specs=[blk(xbc.shape[1]), blk(LANES), blk(width),
                  _const_spec(cw.shape), _const_spec((1, xbc.shape[1])), _const_spec((1, LANES)),
                  _const_spec((1, LANES)), _const_spec((1, width)), _const_spec((1, width)),
                  _const_spec((cs, cs)), _const_spec((LANES, width))],
        out_specs=blk(width),
        out_shape=jax.ShapeDtypeStruct((t, width), BF16),
        scratch_shapes=[pltpu.VMEM((cs + 8, xbc.shape[1]), F32),
                        pltpu.VMEM((SSD_GROUPS, SSD_STATE, width // SSD_GROUPS), F32),
                        pltpu.VMEM((cs, width), F32)],
        compiler_params=_params(("parallel", "arbitrary")),
        name="ssd",
    )(xbc, dt, z, cw, cb[None, :], pad_h(dtb), pad_h(a_neg),
      jnp.repeat(dsk.astype(F32), SSD_HEAD_DIM)[None, :], nw[None, :], tri, expand)


def _rel_bias_tile(tab_ref, head, n_heads, delta, tk, tq):
    key = lax.broadcasted_iota(jnp.int32, (tk, tq), 0)
    qry = lax.broadcasted_iota(jnp.int32, (tk, tq), 1)
    n = jnp.maximum(delta + qry - key, 0)
    max_exact = REL_BUCKETS // 2
    large = max_exact + (jnp.log(jnp.maximum(n, 1).astype(F32) / max_exact)
                         / math.log(REL_MAX_DIST / max_exact) * (REL_BUCKETS - max_exact)).astype(jnp.int32)
    bucket = jnp.where(n < max_exact, n, jnp.minimum(large, REL_BUCKETS - 1))
    far = tab_ref[(REL_BUCKETS - 1) * n_heads + head]
    bias = jnp.zeros((tk, tq), F32)
    for j in range(REL_BUCKETS - 1):
        bias = jnp.where(bucket == j, tab_ref[j * n_heads + head] - far, bias)
    return bias * LOG2E


def _attn_body(tab_ref, lam_ref, qt_ref, k_ref, vt_ref, g_ref, sw_ref, o_ref, bias_ref, *, n_heads, seq):
    tq = tk = ATTN_BLOCK
    nq = seq // tq
    head = pl.program_id(0)

    @pl.when(pl.program_id(1) == 0)
    def _():
        for i, delta in enumerate((0, tq)):
            b = _rel_bias_tile(tab_ref, head, n_heads, delta, tk, tq)
            bias_ref[i] = jnp.concatenate([b, b], axis=1)

    lam = lam_ref[0]
    out_scale = lam_ref[1]
    feat = lax.broadcasted_iota(jnp.int32, (DA_V_DIM, tq), 0)
    first_map = feat < DA_QK_DIM
    key = lax.broadcasted_iota(jnp.int32, (tk, 2 * tq), 0)
    qry = lax.broadcasted_iota(jnp.int32, (tk, 2 * tq), 1)
    causal = jnp.where(qry >= tq, qry - tq, qry) >= key

    def scores(base, qst, k0, bias, masked):
        s = _dot(k_ref[pl.ds(base + k0, tk), :], qst)
        if bias is not None:
            s = s + bias
        if masked:
            s = jnp.where(causal, s, NEG_INF)
        return s

    def update(base, s, k0, carry):
        m, l, acc = carry
        m_new = jnp.maximum(m, jnp.max(s, axis=0, keepdims=True))
        alpha = jnp.exp2(m - m_new)
        p = jnp.exp2(s - m_new)
        v_ext = jnp.concatenate([vt_ref[:, pl.ds(base + k0, tk)], jnp.ones((ATTN_ONES_ROWS, tk), BF16)], axis=0)
        pv = _dot(v_ext, p.astype(BF16))
        l = alpha * l + pv[DA_V_DIM:DA_V_DIM + 1]
        acc = alpha * acc + pv[:DA_V_DIM]
        return m_new, l, acc

    def q_block(qi, q0):
        bases = [bi * seq for bi in range(ATTN_BATCH)]
        qsts = []
        for base in bases:
            qb = qt_ref[:, pl.ds(base + q0, tq)]
            zero = jnp.zeros_like(qb)
            qsts.append(jnp.concatenate([jnp.where(first_map, qb, zero), jnp.where(first_map, zero, qb)], axis=1))
        init = (jnp.full((1, 2 * tq), NEG_INF, F32), jnp.zeros((1, 2 * tq), F32),
                jnp.zeros((DA_V_DIM, 2 * tq), F32))

        def group(tiles, carries):
            ss = [[scores(base, qst, *tile) for base, qst in zip(bases, qsts)] for tile in tiles]
            for (k0, _, _), s_tile in zip(tiles, ss):
                carries = tuple(update(base, s, k0, cr) for base, s, cr in zip(bases, s_tile, carries))
            return carries

        far = lambda ki: (pl.multiple_of(ki * tk, tk), None, False)
        carries = (init,) * ATTN_BATCH
        if qi is None:
            carries = group([(q0, bias_ref[0], True)], carries)
        else:
            n_far = qi - 1
            carries = lax.fori_loop(
                0, n_far // 2, lambda kk, cr: group([far(2 * kk), far(2 * kk + 1)], cr), carries)
            carries = lax.cond(n_far % 2 == 1, lambda cr: group([far(n_far - 1)], cr), lambda cr: cr, carries)
            carries = group([(pl.multiple_of(q0 - tk, tk), bias_ref[1], False), (q0, bias_ref[0], True)], carries)
        for base, (_, l, acc) in zip(bases, carries):
            o_t = acc[:, :tq] / l[:, :tq] - lam * (acc[:, tq:] / l[:, tq:])
            o = o_t.T
            o = o * _rms_scale(o) * sw_ref[...] * out_scale
            o_ref[pl.ds(base + q0, tq), :] = (o * _silu(g_ref[pl.ds(base + q0, tq), :])).astype(BF16)

    q_block(None, 0)

    def later_block(qi, _):
        q_block(qi, pl.multiple_of(qi * tq, tq))
        return 0

    lax.fori_loop(1, nq, later_block, 0)


def _diff_attn(qt, k, vt, g, table, lam_pair, sw, batch, seq):
    t, width = k.shape
    n_heads = width // DA_V_DIM
    blk = pl.BlockSpec((ATTN_BATCH * seq, DA_V_DIM), lambda h, b: (b, h))
    blk_t = pl.BlockSpec((DA_V_DIM, ATTN_BATCH * seq), lambda h, b: (h, b))
    smem = pl.BlockSpec(memory_space=pltpu.SMEM)
    return pl.pallas_call(
        functools.partial(_attn_body, n_heads=n_heads, seq=seq),
        grid=(n_heads, batch // ATTN_BATCH),
        in_specs=[smem, smem, blk_t, blk, blk_t, blk, _const_spec((1, DA_V_DIM))],
        out_specs=blk,
        out_shape=jax.ShapeDtypeStruct((t, width), BF16),
        scratch_shapes=[pltpu.VMEM((2, ATTN_BLOCK, 2 * ATTN_BLOCK), F32)],
        compiler_params=_params(("arbitrary", "arbitrary")),
        name="diff_attn",
    )(table.reshape(-1), lam_pair, qt, k, vt, g, sw[None, :])


def _hgrn_body(q_ref, k_ref, lf_ref, v_ref, g_ref, nw_ref, tri_ref, lvl_ref, y_ref, st_ref, b_ref):
    @pl.when(pl.program_id(2) == 0)
    def _():
        st_ref[...] = jnp.zeros(st_ref.shape, F32)

    c = q_ref.shape[0]
    dk = HG_HEAD_DIM
    heads = [slice(hi * dk, (hi + 1) * dk) for hi in range(HG_HEADS_PER_STEP)]
    lf_hi, lf_lo = _split_bf16(lf_ref[...], 2)
    b_ref[...] = _dot(tri_ref[...], jnp.concatenate([lf_hi, lf_lo], axis=0))

    outs = []
    for hi, hs in enumerate(heads):
        outs.append(_dot_nt((q_ref[:, hs] * jnp.exp2(b_ref[:, hs])).astype(BF16), st_ref[hi].astype(BF16)))
    for hi, hs in enumerate(heads):
        b_last = b_ref[c - 1:c, hs]
        kd = (k_ref[:, hs] * jnp.exp2(b_last - b_ref[:, hs])).astype(BF16)
        st_ref[hi] = st_ref[hi] * jnp.exp2(b_last) + _dot_tn(v_ref[:, hs], kd)

    row = lax.broadcasted_iota(jnp.int32, (c, dk), 0)
    lvl = lvl_ref[...]
    atts = [jnp.where(lvl == 0, jnp.sum(q_ref[:, hs] * k_ref[:, hs], axis=-1, keepdims=True), 0.0)
            for hs in heads]
    m = c // 2
    while m >= 1:
        is_query = (row & m) != 0
        zs = []
        for hs in heads:
            b = b_ref[:, hs]
            ref_rows = jnp.concatenate(
                [jnp.broadcast_to(b[u + m - 1:u + m, :], (2 * m, dk)) for u in range(0, c, 2 * m)], axis=0)
            d = b - ref_rows
            zs.append((jnp.where(is_query, q_ref[:, hs], k_ref[:, hs])
                       * jnp.exp2(jnp.where(is_query, d, -d))).astype(BF16))
        atts = [jnp.where(lvl == m, _dot_nt(z, z), att) for z, att in zip(zs, atts)]
        m //= 2

    for hi, hs in enumerate(heads):
        o = outs[hi] + _dot(atts[hi].astype(BF16), v_ref[:, hs])
        o = o * _rms_scale(o) * nw_ref[...]
        y_ref[:, hs] = (o * _silu(g_ref[:, hs])).astype(BF16)


def _hgrn(q, k, lf, v, g, nw, batch, seq):
    t, width = q.shape
    c = HG_CHUNK
    nc = seq // c
    bw = HG_HEADS_PER_STEP * HG_HEAD_DIM
    idx = jnp.arange(c)
    tri = (idx[:, None] >= idx[None, :]).astype(BF16)
    tri2 = jnp.concatenate([tri, tri], axis=1)
    diff = idx[:, None] ^ idx[None, :]
    top_bit = jnp.left_shift(1, jnp.maximum(31 - lax.clz(diff), 0))
    lvl = jnp.where(idx[:, None] > idx[None, :], top_bit, jnp.where(diff == 0, 0, -1)).astype(jnp.int32)
    blk = pl.BlockSpec((c, bw), lambda b, hg, ci: (b * nc + ci, hg))
    return pl.pallas_call(
        _hgrn_body,
        grid=(batch, width // bw, nc),
        in_specs=[blk, blk, blk, blk, blk, _const_spec((1, HG_HEAD_DIM)), _const_spec((c, 2 * c)),
                  _const_spec((c, c))],
        out_specs=blk,
        out_shape=jax.ShapeDtypeStruct((t, width), BF16),
        scratch_shapes=[pltpu.VMEM((HG_HEADS_PER_STEP, HG_HEAD_DIM, HG_HEAD_DIM), F32),
                        pltpu.VMEM((c, bw), F32)],
        compiler_params=_params(("parallel", "parallel", "arbitrary")),
        name="hgrn",
    )(q, k, lf, v, g, nw[None, :], tri2, lvl)


def kernel(x, norm_w, final_norm_w, rel_bias, even_w_in, even_w_out, conv_w, conv_b, dt_bias, A_log, D_skip,
           ssd_norm_w, lambda_q1, lambda_k1, lambda_q2, lambda_k2, subln_w, odd_w_in, odd_w_out,
           hgrn_lower_bounds, hgrn_norm_w):
    batch, seq, d = x.shape
    depth = norm_w.shape[0]
    tm = PROJ_ROWS
    h = x.reshape(batch * seq, d)

    heads = A_log.shape[1]
    ssd_w = heads * SSD_HEAD_DIM
    conv_ch = conv_w.shape[2]
    da_w = rel_bias.shape[1] * DA_V_DIM
    offs = [0]
    for n in (ssd_w, conv_ch, heads, da_w, da_w, da_w, da_w):
        offs.append(offs[-1] + n)

    lb_all = jax.nn.softmax(hgrn_lower_bounds.astype(F32), axis=0)
    lb_all = jnp.cumsum(lb_all, axis=0) - lb_all[0]

    for layer in range(depth):
        nw = norm_w[layer][None, :]
        last = layer == depth - 1
        if layer % 2 == 0:
            e = layer // 2
            w = even_w_in[e]
            seg = lambda i: w[:, offs[i]:offs[i + 1]]
            w_perm = jnp.concatenate(
                [seg(0), seg(1), seg(4), seg(6), jnp.pad(seg(2), ((0, 0), (0, LANES - heads)))],
                axis=1).astype(BF16)
            w_t = jnp.concatenate([seg(3) * (DA_QK_DIM ** -0.5 * LOG2E), seg(5)], axis=1).T.astype(BF16)
            z, xbc, k, g, dt, q_t, v_t = _proj_even(h, nw, w_perm, w_t, tm)
            y_a = _ssd(xbc, dt, z, conv_w[e], conv_b[e], dt_bias[e], -jnp.exp(A_log[e].astype(F32)) * LOG2E,
                       D_skip[e], ssd_norm_w[e], batch, seq)
            lam_init = 0.8 - 0.6 * math.exp(-0.3 * layer)
            lam = (jnp.exp(jnp.sum(lambda_q1[e].astype(F32) * lambda_k1[e].astype(F32)))
                   - jnp.exp(jnp.sum(lambda_q2[e].astype(F32) * lambda_k2[e].astype(F32))) + lam_init)
            lam_pair = jnp.stack([lam, jnp.asarray(1.0 - lam_init, F32)])
            y_b = _diff_attn(q_t, k, v_t, g, rel_bias.astype(F32), lam_pair, subln_w[e], batch, seq)
            w_out = even_w_out[e].astype(BF16)
            h = _out_proj(h, [y_a, y_b], [w_out[:ssd_w], w_out[ssd_w:]], tm,
                          final_norm_w[None, :] if last else None)
        else:
            o = layer // 2
            q, k, lf, v, g = _proj_odd(h, nw, odd_w_in[o].astype(BF16), lb_all[layer][None, :], tm // 2)
            y = _hgrn(q, k, lf, v, g, hgrn_norm_w[o], batch, seq)
            h = _out_proj(h, [y], [odd_w_out[o].astype(BF16)], tm, final_norm_w[None, :] if last else None)
    return h.reshape(batch, seq, d)
```

```python
import functools
import math

import jax
import jax.numpy as jnp
from jax import lax
from jax.experimental import pallas as pl
from jax.experimental.pallas import tpu as pltpu

F32 = jnp.float32
BF16 = jnp.bfloat16
EPS = 1e-6
NEG_INF = float("-inf")
LOG2E = math.log2(math.e)

LANES = 128
SSD_HEAD_DIM = 64
SSD_GROUPS = 4
SSD_STATE = 128
SSD_CONV = 4
SSD_CHUNK = 128
DA_QK_DIM = 64
DA_V_DIM = 128
REL_BUCKETS = 32
REL_MAX_DIST = 128
HG_HEAD_DIM = 128
HG_CHUNK = 128
HG_HEADS_PER_STEP = 16
ATTN_BLOCK = 256
ATTN_BATCH = 4
ATTN_ONES_ROWS = 16
PROJ_ROWS = 512
VMEM_LIMIT = 56 * 1024 * 1024


def _dot(a, b):
    return jnp.dot(a, b, preferred_element_type=F32)


def _dot_nt(a, b):
    return lax.dot_general(a, b, (((1,), (1,)), ((), ())), preferred_element_type=F32)


def _dot_tn(a, b):
    return lax.dot_general(a, b, (((0,), (0,)), ((), ())), preferred_element_type=F32)


def _split_bf16(x, parts):
    out = []
    r = x
    for _ in range(parts):
        p = r.astype(BF16)
        out.append(p)
        r = r - p.astype(F32)
    return out


def _dot_exact_rhs(x, m_bf16, parts):
    return sum(_dot(p, m_bf16) for p in _split_bf16(x, parts))


def _dot_exact_lhs(m_bf16, x, parts):
    return sum(_dot(m_bf16, p) for p in _split_bf16(x, parts))


def _sigmoid(x):
    return 1.0 / (1.0 + jnp.exp(-x))


def _silu(x):
    return x * _sigmoid(x)


def _rms_scale(x):
    return lax.rsqrt(jnp.mean(x * x, axis=-1, keepdims=True) + EPS)


def _const_spec(shape):
    nd = len(shape)
    return pl.BlockSpec(shape, lambda *_: (0,) * nd, pipeline_mode=pl.Buffered(1))


def _params(sem):
    return pltpu.CompilerParams(dimension_semantics=sem, vmem_limit_bytes=VMEM_LIMIT)


def _proj_even_body(h_ref, nw_ref, w_ref, wt_ref, z_ref, xbc_ref, k_ref, g_ref, dt_ref, qt_ref, vt_ref):
    x = h_ref[...]
    u = (x * _rms_scale(x) * nw_ref[...]).astype(BF16)
    col = 0
    for ref in (z_ref, xbc_ref, k_ref, g_ref, dt_ref):
        n = ref.shape[1]
        step = min(n, 512)
        for j in range(0, n, step):
            ref[:, j:j + step] = _dot(u, w_ref[:, col + j:col + j + step]).astype(ref.dtype)
        col += n
    row = 0
    for ref in (qt_ref, vt_ref):
        n = ref.shape[0]
        for j in range(0, n, 256):
            ref[j:j + 256, :] = _dot_nt(wt_ref[row + j:row + j + 256, :], u).astype(ref.dtype)
        row += n


def _proj_even(h, nw, w, wt, tm):
    t, d = h.shape
    widths = (d, 2 * d, d, d, LANES)
    dtypes = (F32, F32, BF16, F32, F32)
    row = lambda n: pl.BlockSpec((tm, n), lambda i: (i, 0))
    col = pl.BlockSpec((d, tm), lambda i: (0, i))
    return pl.pallas_call(
        _proj_even_body,
        grid=(t // tm,),
        in_specs=[row(d), _const_spec((1, d)), _const_spec(w.shape), _const_spec(wt.shape)],
        out_specs=[row(n) for n in widths] + [col, col],
        out_shape=[jax.ShapeDtypeStruct((t, n), dt) for n, dt in zip(widths, dtypes)]
        + [jax.ShapeDtypeStruct((d, t), BF16)] * 2,
        compiler_params=_params(("parallel",)),
        name="proj_even",
    )(h, nw, w, wt)


def _proj_odd_body(h_ref, nw_ref, w_ref, lb_ref, q_ref, k_ref, lf_ref, v_ref, g_ref):
    x = h_ref[...]
    u = (x * _rms_scale(x) * nw_ref[...]).astype(BF16)
    n = q_ref.shape[1]
    step = 512
    for j in range(0, n, step):
        cs = slice(j, j + step)
        q_ref[:, cs] = _silu(_dot(u, w_ref[:, j:j + step]))
        lb = lb_ref[:, cs]
        f = lb + (1.0 - lb) * _sigmoid(_dot(u, w_ref[:, n + j:n + j + step]))
        lf_ref[:, cs] = jnp.log(f) * LOG2E
        k_ref[:, cs] = 1.0 - f
        v_ref[:, cs] = _dot(u, w_ref[:, 2 * n + j:2 * n + j + step]).astype(BF16)
        g_ref[:, cs] = _dot(u, w_ref[:, 3 * n + j:3 * n + j + step])


def _proj_odd(h, nw, w, lb, tm):
    t, d = h.shape
    n = w.shape[1] // 4
    dtypes = (F32, F32, F32, BF16, F32)
    row = lambda m: pl.BlockSpec((tm, m), lambda i: (i, 0))
    return pl.pallas_call(
        _proj_odd_body,
        grid=(t // tm,),
        in_specs=[row(d), _const_spec((1, d)), _const_spec(w.shape), _const_spec((1, n))],
        out_specs=[row(n) for _ in dtypes],
        out_shape=[jax.ShapeDtypeStruct((t, n), dt) for dt in dtypes],
        compiler_params=_params(("parallel",)),
        name="proj_odd",
    )(h, nw, w, lb)


def _out_proj_body(*refs, n_act, final):
    h_ref = refs[0]
    acts = refs[1:1 + n_act]
    ws = refs[1 + n_act:1 + 2 * n_act]
    rest = refs[1 + 2 * n_act:]
    acc = h_ref[...]
    for a_ref, w_ref in zip(acts, ws):
        acc = acc + _dot(a_ref[...], w_ref[...])
    if final:
        fw_ref, o_ref = rest
        acc = acc * _rms_scale(acc) * fw_ref[...]
    else:
        (o_ref,) = rest
    o_ref[...] = acc


def _out_proj(h, acts, ws, tm, final_w=None):
    t, d = h.shape
    final = final_w is not None
    row = lambda n: pl.BlockSpec((tm, n), lambda i: (i, 0))
    in_specs = [row(d)] + [row(a.shape[1]) for a in acts] + [_const_spec(w.shape) for w in ws]
    args = [h, *acts, *ws]
    if final:
        in_specs.append(_const_spec((1, d)))
        args.append(final_w)
    return pl.pallas_call(
        functools.partial(_out_proj_body, n_act=len(acts), final=final),
        grid=(t // tm,),
        in_specs=in_specs,
        out_specs=row(d),
        out_shape=jax.ShapeDtypeStruct((t, d), F32),
        compiler_params=_params(("parallel",)),
        name="out_proj",
    )(*args)


def _softplus(x):
    return jnp.maximum(x, 0.0) + jnp.log(1.0 + jnp.exp(-jnp.abs(x)))


def _ssd_body(xbc_ref, dt_ref, z_ref, cw_ref, cb_ref, dtb_ref, a_ref, dsk_ref, nw_ref, tri_ref,
              exp_ref, y_ref, ext_ref, st_ref, yacc_ref, *, width):
    cs = SSD_CHUNK
    gw = width // SSD_GROUPS
    hpg = gw // SSD_HEAD_DIM
    pad = 8
    c = pl.program_id(1)

    @pl.when(c == 0)
    def _():
        ext_ref[0:pad, :] = jnp.zeros((pad, ext_ref.shape[1]), F32)
        st_ref[...] = jnp.zeros(st_ref.shape, F32)

    xin = xbc_ref[...]
    ext_ref[pad:pad + cs, :] = xin
    conv = cb_ref[...] + cw_ref[SSD_CONV - 1:SSD_CONV, :] * xin
    for j in range(1, SSD_CONV):
        conv = conv + cw_ref[SSD_CONV - 1 - j:SSD_CONV - j, :] * ext_ref[pad - j:pad - j + cs, :]
    ext_ref[0:pad, :] = xin[cs - pad:cs, :]
    act = _silu(conv)

    dt = _softplus(dt_ref[...] + dtb_ref[...])
    a = dt * a_ref[...]
    a_cs = _dot_exact_lhs(tri_ref[...], a, 3)
    a_cs_t = a_cs.T
    dt_t = dt.T
    a_last = a_cs[cs - 1:cs, :]
    per_head = jnp.concatenate(
        [dt * jnp.exp2(a_last - a_cs), jnp.exp2(a_cs), jnp.broadcast_to(jnp.exp2(a_last), (8, LANES))], axis=0)
    per_col = _dot_exact_rhs(per_head, exp_ref[...], 2)
    w_state = per_col[0:cs]
    e_acs = per_col[cs:2 * cs]
    e_last = per_col[2 * cs:2 * cs + 1]

    row = lax.broadcasted_iota(jnp.int32, (cs, cs), 0)
    colm = lax.broadcasted_iota(jnp.int32, (cs, cs), 1)
    tril = row >= colm
    lane = lax.broadcasted_iota(jnp.int32, (cs, LANES), 1)
    low_half = lane < SSD_HEAD_DIM

    group_cols = [slice(g * gw, (g + 1) * gw) for g in range(SSD_GROUPS)]
    cbs, y_offs = [], []
    for g, gs in enumerate(group_cols):
        bg = act[:, width + g * SSD_STATE:width + (g + 1) * SSD_STATE].astype(BF16)
        cg = act[:, width + (SSD_GROUPS + g) * SSD_STATE:width + (SSD_GROUPS + g + 1) * SSD_STATE].astype(BF16)
        cbs.append(_dot_nt(cg, bg))
        s_prev = st_ref[g]
        y_offs.append(_dot(cg, s_prev.astype(BF16)))
        xd = (act[:, gs] * w_state[:, gs]).astype(BF16)
        st_ref[g] = s_prev * e_last[:, gs] + _dot_tn(bg, xd)
    for g, gs in enumerate(group_cols):
        xg = act[:, gs]
        y_off = y_offs[g] * e_acs[:, gs]
        for p in range(gw // LANES):
            xp = xg[:, p * LANES:(p + 1) * LANES]
            halves = (jnp.where(low_half, xp, 0.0).astype(BF16), jnp.where(low_half, 0.0, xp).astype(BF16))
            acc = y_off[:, p * LANES:(p + 1) * LANES] + xp * dsk_ref[:, g * gw + p * LANES:g * gw + (p + 1) * LANES]
            for r in range(LANES // SSD_HEAD_DIM):
                hd = g * hpg + p * (LANES // SSD_HEAD_DIM) + r
                decay = jnp.exp2(jnp.where(tril, a_cs[:, hd:hd + 1] - a_cs_t[hd:hd + 1, :], NEG_INF))
                scores = (cbs[g] * decay * dt_t[hd:hd + 1, :]).astype(BF16)
                acc = acc + _dot(scores, halves[r])
            yacc_ref[:, g * gw + p * LANES:g * gw + (p + 1) * LANES] = acc

    yz = yacc_ref[...] * _silu(z_ref[...])
    y_ref[...] = (yz * _rms_scale(yz) * nw_ref[...]).astype(BF16)


def _ssd(xbc, dt, z, cw, cb, dtb, a_neg, dsk, nw, batch, seq):
    t, width = z.shape
    cs = SSD_CHUNK
    nc = seq // cs
    heads = width // SSD_HEAD_DIM
    tri = (jnp.arange(cs)[:, None] >= jnp.arange(cs)[None, :]).astype(BF16)
    expand = (jnp.arange(LANES)[:, None] == (jnp.arange(width)[None, :] // SSD_HEAD_DIM)).astype(BF16)
    pad_h = lambda v: jnp.pad(v.astype(F32), (0, LANES - heads))[None, :]
    blk = lambda n: pl.BlockSpec((cs, n), lambda b, c: (b * nc + c, 0))
    return pl.pallas_call(
        functools.partial(_ssd_body, width=width),
        grid=(batch, nc),
        in_specs=[blk(xbc.shape[1]), blk(LANES), blk(width),
                  _const_spec(cw.shape), _const_spec((1, xbc.shape[1])), _const_spec((1, LANES)),
                  _const_spec((1, LANES)), _const_spec((1, width)), _const_spec((1, width)),
                  _const_spec((cs, cs)), _const_spec((LANES, width))],
        out_specs=blk(width),
        out_shape=jax.ShapeDtypeStruct((t, width), BF16),
        scratch_shapes=[pltpu.VMEM((cs + 8, xbc.shape[1]), F32),
                        pltpu.VMEM((SSD_GROUPS, SSD_STATE, width // SSD_GROUPS), F32),
                        pltpu.VMEM((cs, width), F32)],
        compiler_params=_params(("parallel", "arbitrary")),
        name="ssd",
    )(xbc, dt, z, cw, cb[None, :], pad_h(dtb), pad_h(a_neg),
      jnp.repeat(dsk.astype(F32), SSD_HEAD_DIM)[None, :], nw[None, :], tri, expand)


def _rel_bias_tile(tab_ref, head, n_heads, delta, tk, tq):
    key = lax.broadcasted_iota(jnp.int32, (tk, tq), 0)
    qry = lax.broadcasted_iota(jnp.int32, (tk, tq), 1)
    n = jnp.maximum(delta + qry - key, 0)
    max_exact = REL_BUCKETS // 2
    large = max_exact + (jnp.log(jnp.maximum(n, 1).astype(F32) / max_exact)
                         / math.log(REL_MAX_DIST / max_exact) * (REL_BUCKETS - max_exact)).astype(jnp.int32)
    bucket = jnp.where(n < max_exact, n, jnp.minimum(large, REL_BUCKETS - 1))
    far = tab_ref[(REL_BUCKETS - 1) * n_heads + head]
    bias = jnp.zeros((tk, tq), F32)
    for j in range(REL_BUCKETS - 1):
        bias = jnp.where(bucket == j, tab_ref[j * n_heads + head] - far, bias)
    return bias * LOG2E


def _attn_body(tab_ref, lam_ref, qt_ref, k_ref, vt_ref, g_ref, sw_ref, o_ref, bias_ref, *, n_heads, seq):
    tq = tk = ATTN_BLOCK
    nq = seq // tq
    head = pl.program_id(0)

    @pl.when(pl.program_id(1) == 0)
    def _():
        for i, delta in enumerate((0, tq)):
            b = _rel_bias_tile(tab_ref, head, n_heads, delta, tk, tq)
            bias_ref[i] = jnp.concatenate([b, b], axis=1)

    lam = lam_ref[0]
    out_scale = lam_ref[1]
    feat = lax.broadcasted_iota(jnp.int32, (DA_V_DIM, tq), 0)
    first_map = feat < DA_QK_DIM
    key = lax.broadcasted_iota(jnp.int32, (tk, 2 * tq), 0)
    qry = lax.broadcasted_iota(jnp.int32, (tk, 2 * tq), 1)
    causal = jnp.where(qry >= tq, qry - tq, qry) >= key

    def scores(base, qst, k0, bias, masked):
        s = _dot(k_ref[pl.ds(base + k0, tk), :], qst)
        if bias is not None:
            s = s + bias
        if masked:
            s = jnp.where(causal, s, NEG_INF)
        return s

    def update(base, s, k0, carry):
        m, l, acc = carry
        m_new = jnp.maximum(m, jnp.max(s, axis=0, keepdims=True))
        alpha = jnp.exp2(m - m_new)
        p = jnp.exp2(s - m_new)
        v_ext = jnp.concatenate([vt_ref[:, pl.ds(base + k0, tk)], jnp.ones((ATTN_ONES_ROWS, tk), BF16)], axis=0)
        pv = _dot(v_ext, p.astype(BF16))
        l = alpha * l + pv[DA_V_DIM:DA_V_DIM + 1]
        acc = alpha * acc + pv[:DA_V_DIM]
        return m_new, l, acc

    def q_block(qi, q0):
        bases = [bi * seq for bi in range(ATTN_BATCH)]
        qsts = []
        for base in bases:
            qb = qt_ref[:, pl.ds(base + q0, tq)]
            zero = jnp.zeros_like(qb)
            qsts.append(jnp.concatenate([jnp.where(first_map, qb, zero), jnp.where(first_map, zero, qb)], axis=1))
        init = (jnp.full((1, 2 * tq), NEG_INF, F32), jnp.zeros((1, 2 * tq), F32),
                jnp.zeros((DA_V_DIM, 2 * tq), F32))

        def group(tiles, carries):
            ss = [[scores(base, qst, *tile) for base, qst in zip(bases, qsts)] for tile in tiles]
            for (k0, _, _), s_tile in zip(tiles, ss):
                carries = tuple(update(base, s, k0, cr) for base, s, cr in zip(bases, s_tile, carries))
            return carries

        far = lambda ki: (pl.multiple_of(ki * tk, tk), None, False)
        carries = (init,) * ATTN_BATCH
        if qi is None:
            carries = group([(q0, bias_ref[0], True)], carries)
        else:
            n_far = qi - 1
            carries = lax.fori_loop(
                0, n_far // 2, lambda kk, cr: group([far(2 * kk), far(2 * kk + 1)], cr), carries)
            last_two = [(pl.multiple_of(q0 - tk, tk), bias_ref[1], False), (q0, bias_ref[0], True)]
            carries = lax.cond(n_far % 2 == 1, lambda cr: group([far(n_far - 1)] + last_two, cr),
                               lambda cr: group(last_two, cr), carries)
        for base, (_, l, acc) in zip(bases, carries):
            o_t = acc[:, :tq] / l[:, :tq] - lam * (acc[:, tq:] / l[:, tq:])
            o = o_t.T
            o = o * _rms_scale(o) * sw_ref[...] * out_scale
            o_ref[pl.ds(base + q0, tq), :] = (o * _silu(g_ref[pl.ds(base + q0, tq), :])).astype(BF16)

    q_block(None, 0)

    def later_block(qi, _):
        q_block(qi, pl.multiple_of(qi * tq, tq))
        return 0

    lax.fori_loop(1, nq, later_block, 0)


def _diff_attn(qt, k, vt, g, table, lam_pair, sw, batch, seq):
    t, width = k.shape
    n_heads = width // DA_V_DIM
    blk = pl.BlockSpec((ATTN_BATCH * seq, DA_V_DIM), lambda h, b: (b, h))
    blk_t = pl.BlockSpec((DA_V_DIM, ATTN_BATCH * seq), lambda h, b: (h, b))
    smem = pl.BlockSpec(memory_space=pltpu.SMEM)
    return pl.pallas_call(
        functools.partial(_attn_body, n_heads=n_heads, seq=seq),
        grid=(n_heads, batch // ATTN_BATCH),
        in_specs=[smem, smem, blk_t, blk, blk_t, blk, _const_spec((1, DA_V_DIM))],
        out_specs=blk,
        out_shape=jax.ShapeDtypeStruct((t, width), BF16),
        scratch_shapes=[pltpu.VMEM((2, ATTN_BLOCK, 2 * ATTN_BLOCK), F32)],
        compiler_params=_params(("arbitrary", "arbitrary")),
        name="diff_attn",
    )(table.reshape(-1), lam_pair, qt, k, vt, g, sw[None, :])


def _hgrn_body(q_ref, k_ref, lf_ref, v_ref, g_ref, nw_ref, tri_ref, lvl_ref, y_ref, st_ref, b_ref):
    @pl.when(pl.program_id(2) == 0)
    def _():
        st_ref[...] = jnp.zeros(st_ref.shape, F32)

    c = q_ref.shape[0]
    dk = HG_HEAD_DIM
    heads = [slice(hi * dk, (hi + 1) * dk) for hi in range(HG_HEADS_PER_STEP)]
    lf_hi, lf_lo = _split_bf16(lf_ref[...], 2)
    b_ref[...] = _dot(tri_ref[...], jnp.concatenate([lf_hi, lf_lo], axis=0))

    outs = []
    for hi, hs in enumerate(heads):
        outs.append(_dot_nt((q_ref[:, hs] * jnp.exp2(b_ref[:, hs])).astype(BF16), st_ref[hi].astype(BF16)))
    for hi, hs in enumerate(heads):
        b_last = b_ref[c - 1:c, hs]
        kd = (k_ref[:, hs] * jnp.exp2(b_last - b_ref[:, hs])).astype(BF16)
        st_ref[hi] = st_ref[hi] * jnp.exp2(b_last) + _dot_tn(v_ref[:, hs], kd)

    row = lax.broadcasted_iota(jnp.int32, (c, dk), 0)
    lvl = lvl_ref[...]
    atts = [jnp.where(lvl == 0, jnp.sum(q_ref[:, hs] * k_ref[:, hs], axis=-1, keepdims=True), 0.0)
            for hs in heads]
    m = c // 2
    while m >= 1:
        is_query = (row & m) != 0
        zs = []
        for hs in heads:
            b = b_ref[:, hs]
            ref_rows = jnp.concatenate(
                [jnp.broadcast_to(b[u + m - 1:u + m, :], (2 * m, dk)) for u in range(0, c, 2 * m)], axis=0)
            d = b - ref_rows
            zs.append((jnp.where(is_query, q_ref[:, hs], k_ref[:, hs])
                       * jnp.exp2(jnp.where(is_query, d, -d))).astype(BF16))
        atts = [jnp.where(lvl == m, _dot_nt(z, z), att) for z, att in zip(zs, atts)]
        m //= 2

    for hi, hs in enumerate(heads):
        o = outs[hi] + _dot(atts[hi].astype(BF16), v_ref[:, hs])
        o = o * _rms_scale(o) * nw_ref[...]
        y_ref[:, hs] = (o * _silu(g_ref[:, hs])).astype(BF16)


def _hgrn(q, k, lf, v, g, nw, batch, seq):
    t, width = q.shape
    c = HG_CHUNK
    nc = seq // c
    bw = HG_HEADS_PER_STEP * HG_HEAD_DIM
    idx = jnp.arange(c)
    tri = (idx[:, None] >= idx[None, :]).astype(BF16)
    tri2 = jnp.concatenate([tri, tri], axis=1)
    diff = idx[:, None] ^ idx[None, :]
    top_bit = jnp.left_shift(1, jnp.maximum(31 - lax.clz(diff), 0))
    lvl = jnp.where(idx[:, None] > idx[None, :], top_bit, jnp.where(diff == 0, 0, -1)).astype(jnp.int32)
    blk = pl.BlockSpec((c, bw), lambda b, hg, ci: (b * nc + ci, hg))
    return pl.pallas_call(
        _hgrn_body,
        grid=(batch, width // bw, nc),
        in_specs=[blk, blk, blk, blk, blk, _const_spec((1, HG_HEAD_DIM)), _const_spec((c, 2 * c)),
                  _const_spec((c, c))],
        out_specs=blk,
        out_shape=jax.ShapeDtypeStruct((t, width), BF16),
        scratch_shapes=[pltpu.VMEM((HG_HEADS_PER_STEP, HG_HEAD_DIM, HG_HEAD_DIM), F32),
                        pltpu.VMEM((c, bw), F32)],
        compiler_params=_params(("parallel", "parallel", "arbitrary")),
        name="hgrn",
    )(q, k, lf, v, g, nw[None, :], tri2, lvl)


def kernel(x, norm_w, final_norm_w, rel_bias, even_w_in, even_w_out, conv_w, conv_b, dt_bias, A_log, D_skip,
           ssd_norm_w, lambda_q1, lambda_k1, lambda_q2, lambda_k2, subln_w, odd_w_in, odd_w_out,
           hgrn_lower_bounds, hgrn_norm_w):
    batch, seq, d = x.shape
    depth = norm_w.shape[0]
    tm = PROJ_ROWS
    h = x.reshape(batch * seq, d)

    heads = A_log.shape[1]
    ssd_w = heads * SSD_HEAD_DIM
    conv_ch = conv_w.shape[2]
    da_w = rel_bias.shape[1] * DA_V_DIM
    offs = [0]
    for n in (ssd_w, conv_ch, heads, da_w, da_w, da_w, da_w):
        offs.append(offs[-1] + n)

    lb_all = jax.nn.softmax(hgrn_lower_bounds.astype(F32), axis=0)
    lb_all = jnp.cumsum(lb_all, axis=0) - lb_all[0]

    for layer in range(depth):
        nw = norm_w[layer][None, :]
        last = layer == depth - 1
        if layer % 2 == 0:
            e = layer // 2
            w = even_w_in[e]
            seg = lambda i: w[:, offs[i]:offs[i + 1]]
            w_perm = jnp.concatenate(
                [seg(0), seg(1), seg(4), seg(6), jnp.pad(seg(2), ((0, 0), (0, LANES - heads)))],
                axis=1).astype(BF16)
            w_t = jnp.concatenate([seg(3) * (DA_QK_DIM ** -0.5 * LOG2E), seg(5)], axis=1).T.astype(BF16)
            z, xbc, k, g, dt, q_t, v_t = _proj_even(h, nw, w_perm, w_t, tm)
            y_a = _ssd(xbc, dt, z, conv_w[e], conv_b[e], dt_bias[e], -jnp.exp(A_log[e].astype(F32)) * LOG2E,
                       D_skip[e], ssd_norm_w[e], batch, seq)
            lam_init = 0.8 - 0.6 * math.exp(-0.3 * layer)
            lam = (jnp.exp(jnp.sum(lambda_q1[e].astype(F32) * lambda_k1[e].astype(F32)))
                   - jnp.exp(jnp.sum(lambda_q2[e].astype(F32) * lambda_k2[e].astype(F32))) + lam_init)
            lam_pair = jnp.stack([lam, jnp.asarray(1.0 - lam_init, F32)])
            y_b = _diff_attn(q_t, k, v_t, g, rel_bias.astype(F32), lam_pair, subln_w[e], batch, seq)
            w_out = even_w_out[e].astype(BF16)
            h = _out_proj(h, [y_a, y_b], [w_out[:ssd_w], w_out[ssd_w:]], tm,
                          final_norm_w[None, :] if last else None)
        else:
            o = layer // 2
            q, k, lf, v, g = _proj_odd(h, nw, odd_w_in[o].astype(BF16), lb_all[layer][None, :], tm // 2)
            y = _hgrn(q, k, lf, v, g, hgrn_norm_w[o], batch, seq)
            h = _out_proj(h, [y], [odd_w_out[o].astype(BF16)], tm, final_norm_w[None, :] if last else None)
    return h.reshape(batch, seq, d)
```

```python
import functools
import math

import jax
import jax.numpy as jnp
from jax import lax
from jax.experimental import pallas as pl
from jax.experimental.pallas import tpu as pltpu

F32 = jnp.float32
BF16 = jnp.bfloat16
EPS = 1e-6
NEG_INF = float("-inf")
LOG2E = math.log2(math.e)

LANES = 128
SSD_HEAD_DIM = 64
SSD_GROUPS = 4
SSD_STATE = 128
SSD_CONV = 4
SSD_CHUNK = 128
DA_QK_DIM = 64
DA_V_DIM = 128
REL_BUCKETS = 32
REL_MAX_DIST = 128
HG_HEAD_DIM = 128
HG_CHUNK = 128
HG_HEADS_PER_STEP = 16
ATTN_BLOCK = 256
ATTN_BATCH = 4
ATTN_ONES_ROWS = 16
PROJ_ROWS = 512
VMEM_LIMIT = 56 * 1024 * 1024


def _dot(a, b):
    return jnp.dot(a, b, preferred_element_type=F32)


def _dot_nt(a, b):
    return lax.dot_general(a, b, (((1,), (1,)), ((), ())), preferred_element_type=F32)


def _dot_tn(a, b):
    return lax.dot_general(a, b, (((0,), (0,)), ((), ())), preferred_element_type=F32)


def _split_bf16(x, parts):
    out = []
    r = x
    for _ in range(parts):
        p = r.astype(BF16)
        out.append(p)
        r = r - p.astype(F32)
    return out


def _dot_exact_rhs(x, m_bf16, parts):
    return sum(_dot(p, m_bf16) for p in _split_bf16(x, parts))


def _dot_exact_lhs(m_bf16, x, parts):
    return sum(_dot(m_bf16, p) for p in _split_bf16(x, parts))


def _sigmoid(x):
    return 1.0 / (1.0 + jnp.exp(-x))


def _silu(x):
    return x * _sigmoid(x)


def _rms_scale(x):
    return lax.rsqrt(jnp.mean(x * x, axis=-1, keepdims=True) + EPS)


def _const_spec(shape):
    nd = len(shape)
    return pl.BlockSpec(shape, lambda *_: (0,) * nd, pipeline_mode=pl.Buffered(1))


def _params(sem):
    return pltpu.CompilerParams(dimension_semantics=sem, vmem_limit_bytes=VMEM_LIMIT)


def _block_input(h_ref, act_refs, wo_refs, hn_ref):
    x = h_ref[...]
    if act_refs:
        for a_ref, w_ref in zip(act_refs, wo_refs):
            x = x + _dot(a_ref[...], w_ref[...])
        hn_ref[...] = x
    return x


def _split_refs(refs, n_act, n_fixed):
    acts, wos = refs[1:1 + n_act], refs[1 + n_act:1 + 2 * n_act]
    fixed = refs[1 + 2 * n_act:1 + 2 * n_act + n_fixed]
    outs = refs[1 + 2 * n_act + n_fixed:]
    hn_ref = None
    if n_act:
        hn_ref, outs = outs[0], outs[1:]
    return refs[0], acts, wos, fixed, hn_ref, outs


def _prev_specs(prev, tm):
    acts, ws = prev if prev is not None else ((), ())
    specs = [pl.BlockSpec((tm, a.shape[1]), lambda i: (i, 0)) for a in acts] + [_const_spec(w.shape) for w in ws]
    return list(acts) + list(ws), specs


def _proj_even_body(*refs, n_act):
    h_ref, acts, wos, (nw_ref, w_ref, wt_ref), hn_ref, outs = _split_refs(refs, n_act, 3)
    z_ref, xbc_ref, k_ref, g_ref, dt_ref, qt_ref, vt_ref = outs
    x = _block_input(h_ref, acts, wos, hn_ref)
    u = (x * _rms_scale(x) * nw_ref[...]).astype(BF16)
    col = 0
    for ref in (z_ref, xbc_ref, k_ref, g_ref, dt_ref):
        n = ref.shape[1]
        step = min(n, 512)
        for j in range(0, n, step):
            ref[:, j:j + step] = _dot(u, w_ref[:, col + j:col + j + step]).astype(ref.dtype)
        col += n
    row = 0
    for ref in (qt_ref, vt_ref):
        n = ref.shape[0]
        for j in range(0, n, 256):
            ref[j:j + 256, :] = _dot_nt(wt_ref[row + j:row + j + 256, :], u).astype(ref.dtype)
        row += n


def _proj_even(h, nw, w, wt, tm, prev=None):
    t, d = h.shape
    widths = (d, 2 * d, d, d, LANES)
    dtypes = (F32, F32, BF16, F32, F32)
    row = lambda n: pl.BlockSpec((tm, n), lambda i: (i, 0))
    col = pl.BlockSpec((d, tm), lambda i: (0, i))
    prev_args, prev_specs = _prev_specs(prev, tm)
    out_specs = [row(n) for n in widths] + [col, col]
    out_shape = ([jax.ShapeDtypeStruct((t, n), dt) for n, dt in zip(widths, dtypes)]
                 + [jax.ShapeDtypeStruct((d, t), BF16)] * 2)
    if prev is not None:
        out_specs, out_shape = [row(d)] + out_specs, [jax.ShapeDtypeStruct((t, d), F32)] + out_shape
    outs = pl.pallas_call(
        functools.partial(_proj_even_body, n_act=len(prev_args) // 2),
        grid=(t // tm,),
        in_specs=[row(d)] + prev_specs + [_const_spec((1, d)), _const_spec(w.shape), _const_spec(wt.shape)],
        out_specs=out_specs,
        out_shape=out_shape,
        compiler_params=_params(("parallel",)),
        name="proj_even",
    )(h, *prev_args, nw, w, wt)
    return tuple(outs) if prev is not None else (h, *outs)


def _proj_odd_body(*refs, n_act):
    h_ref, acts, wos, (nw_ref, w_ref, lb_ref), hn_ref, outs = _split_refs(refs, n_act, 3)
    q_ref, k_ref, lf_ref, v_ref, g_ref = outs
    x = _block_input(h_ref, acts, wos, hn_ref)
    u = (x * _rms_scale(x) * nw_ref[...]).astype(BF16)
    n = q_ref.shape[1]
    step = 512
    for j in range(0, n, step):
        cs = slice(j, j + step)
        q_ref[:, cs] = _silu(_dot(u, w_ref[:, j:j + step]))
        lb = lb_ref[:, cs]
        f = lb + (1.0 - lb) * _sigmoid(_dot(u, w_ref[:, n + j:n + j + step]))
        lf_ref[:, cs] = jnp.log(f) * LOG2E
        k_ref[:, cs] = 1.0 - f
        v_ref[:, cs] = _dot(u, w_ref[:, 2 * n + j:2 * n + j + step]).astype(BF16)
        g_ref[:, cs] = _dot(u, w_ref[:, 3 * n + j:3 * n + j + step])


def _proj_odd(h, nw, w, lb, tm, prev=None):
    t, d = h.shape
    n = w.shape[1] // 4
    dtypes = (F32, F32, F32, BF16, F32)
    row = lambda m: pl.BlockSpec((tm, m), lambda i: (i, 0))
    prev_args, prev_specs = _prev_specs(prev, tm)
    out_specs = [row(n) for _ in dtypes]
    out_shape = [jax.ShapeDtypeStruct((t, n), dt) for dt in dtypes]
    if prev is not None:
        out_specs, out_shape = [row(d)] + out_specs, [jax.ShapeDtypeStruct((t, d), F32)] + out_shape
    outs = pl.pallas_call(
        functools.partial(_proj_odd_body, n_act=len(prev_args) // 2),
        grid=(t // tm,),
        in_specs=[row(d)] + prev_specs + [_const_spec((1, d)), _const_spec(w.shape), _const_spec((1, n))],
        out_specs=out_specs,
        out_shape=out_shape,
        compiler_params=_params(("parallel",)),
        name="proj_odd",
    )(h, *prev_args, nw, w, lb)
    return tuple(outs) if prev is not None else (h, *outs)


def _out_proj_body(*refs, n_act):
    h_ref, acts, wos, (fw_ref,), o_ref, _ = _split_refs(refs, n_act, 1)
    x = h_ref[...]
    for a_ref, w_ref in zip(acts, wos):
        x = x + _dot(a_ref[...], w_ref[...])
    o_ref[...] = x * _rms_scale(x) * fw_ref[...]


def _out_proj(h, acts, ws, tm, final_w):
    t, d = h.shape
    row = lambda n: pl.BlockSpec((tm, n), lambda i: (i, 0))
    prev_args, prev_specs = _prev_specs((acts, ws), tm)
    return pl.pallas_call(
        functools.partial(_out_proj_body, n_act=len(acts)),
        grid=(t // tm,),
        in_specs=[row(d)] + prev_specs + [_const_spec((1, d))],
        out_specs=row(d),
        out_shape=jax.ShapeDtypeStruct((t, d), F32),
        compiler_params=_params(("parallel",)),
        name="out_proj",
    )(h, *prev_args, final_w)


def _softplus(x):
    return jnp.maximum(x, 0.0) + jnp.log(1.0 + jnp.exp(-jnp.abs(x)))


def _ssd_body(xbc_ref, dt_ref, z_ref, cw_ref, cb_ref, dtb_ref, a_ref, dsk_ref, nw_ref, tri_ref,
              exp_ref, y_ref, ext_ref, st_ref, yacc_ref, *, width):
    cs = SSD_CHUNK
    gw = width // SSD_GROUPS
    hpg = gw // SSD_HEAD_DIM
    pad = 8
    c = pl.program_id(1)

    @pl.when(c == 0)
    def _():
        ext_ref[0:pad, :] = jnp.zeros((pad, ext_ref.shape[1]), F32)
        st_ref[...] = jnp.zeros(st_ref.shape, F32)

    xin = xbc_ref[...]
    ext_ref[pad:pad + cs, :] = xin
    conv = cb_ref[...] + cw_ref[SSD_CONV - 1:SSD_CONV, :] * xin
    for j in range(1, SSD_CONV):
        conv = conv + cw_ref[SSD_CONV - 1 - j:SSD_CONV - j, :] * ext_ref[pad - j:pad - j + cs, :]
    ext_ref[0:pad, :] = xin[cs - pad:cs, :]
    act = _silu(conv)

    dt = _softplus(dt_ref[...] + dtb_ref[...])
    a = dt * a_ref[...]
    a_cs = _dot_exact_lhs(tri_ref[...], a, 3)
    a_cs_t = a_cs.T
    dt_t = dt.T
    a_last = a_cs[cs - 1:cs, :]
    per_head = jnp.concatenate(
        [dt * jnp.exp2(a_last - a_cs), jnp.exp2(a_cs), jnp.broadcast_to(jnp.exp2(a_last), (8, LANES))], axis=0)
    per_col = _dot_exact_rhs(per_head, exp_ref[...], 2)
    w_state = per_col[0:cs]
    e_acs = per_col[cs:2 * cs]
    e_last = per_col[2 * cs:2 * cs + 1]

    row = lax.broadcasted_iota(jnp.int32, (cs, cs), 0)
    colm = lax.broadcasted_iota(jnp.int32, (cs, cs), 1)
    tril = row >= colm
    lane = lax.broadcasted_iota(jnp.int32, (cs, LANES), 1)
    low_half = lane < SSD_HEAD_DIM

    group_cols = [slice(g * gw, (g + 1) * gw) for g in range(SSD_GROUPS)]
    cbs, y_offs = [], []
    for g, gs in enumerate(group_cols):
        bg = act[:, width + g * SSD_STATE:width + (g + 1) * SSD_STATE].astype(BF16)
        cg = act[:, width + (SSD_GROUPS + g) * SSD_STATE:width + (SSD_GROUPS + g + 1) * SSD_STATE].astype(BF16)
        cbs.append(_dot_nt(cg, bg))
        s_prev = st_ref[g]
        y_offs.append(_dot(cg, s_prev.astype(BF16)))
        xd = (act[:, gs] * w_state[:, gs]).astype(BF16)
        st_ref[g] = s_prev * e_last[:, gs] + _dot_tn(bg, xd)
    for g, gs in enumerate(group_cols):
        xg = act[:, gs]
        y_off = y_offs[g] * e_acs[:, gs]
        for p in range(gw // LANES):
            xp = xg[:, p * LANES:(p + 1) * LANES]
            halves = (jnp.where(low_half, xp, 0.0).astype(BF16), jnp.where(low_half, 0.0, xp).astype(BF16))
            acc = y_off[:, p * LANES:(p + 1) * LANES] + xp * dsk_ref[:, g * gw + p * LANES:g * gw + (p + 1) * LANES]
            for r in range(LANES // SSD_HEAD_DIM):
                hd = g * hpg + p * (LANES // SSD_HEAD_DIM) + r
                decay = jnp.exp2(jnp.where(tril, a_cs[:, hd:hd + 1] - a_cs_t[hd:hd + 1, :], NEG_INF))
                scores = (cbs[g] * decay * dt_t[hd:hd + 1, :]).astype(BF16)
                acc = acc + _dot(scores, halves[r])
            yacc_ref[:, g * gw + p * LANES:g * gw + (p + 1) * LANES] = acc

    yz = yacc_ref[...] * _silu(z_ref[...])
    y_ref[...] = (yz * _rms_scale(yz) * nw_ref[...]).astype(BF16)


def _ssd(xbc, dt, z, cw, cb, dtb, a_neg, dsk, nw, batch, seq):
    t, width = z.shape
    cs = SSD_CHUNK
    nc = seq // cs
    heads = width // SSD_HEAD_DIM
    tri = (jnp.arange(cs)[:, None] >= jnp.arange(cs)[None, :]).astype(BF16)
    expand = (jnp.arange(LANES)[:, None] == (jnp.arange(width)[None, :] // SSD_HEAD_DIM)).astype(BF16)
    pad_h = lambda v: jnp.pad(v.astype(F32), (0, LANES - heads))[None, :]
    blk = lambda n: pl.BlockSpec((cs, n), lambda b, c: (b * nc + c, 0))
    return pl.pallas_call(
        functools.partial(_ssd_body, width=width),
        grid=(batch, nc),
        in_specs=[blk(xbc.shape[1]), blk(LANES), blk(width),
                  _const_spec(cw.shape), _const_spec((1, xbc.shape[1])), _const_spec((1, LANES)),
                  _const_spec((1, LANES)), _const_spec((1, width)), _const_spec((1, width)),
                  _const_spec((cs, cs)), _const_spec((LANES, width))],
        out_specs=blk(width),
        out_shape=jax.ShapeDtypeStruct((t, width), BF16),
        scratch_shapes=[pltpu.VMEM((cs + 8, xbc.shape[1]), F32),
                        pltpu.VMEM((SSD_GROUPS, SSD_STATE, width // SSD_GROUPS), F32),
                        pltpu.VMEM((cs, width), F32)],
        compiler_params=_params(("parallel", "arbitrary")),
        name="ssd",
    )(xbc, dt, z, cw, cb[None, :], pad_h(dtb), pad_h(a_neg),
      jnp.repeat(dsk.astype(F32), SSD_HEAD_DIM)[None, :], nw[None, :], tri, expand)


def _rel_bias_tile(tab_ref, head, n_heads, delta, tk, tq):
    key = lax.broadcasted_iota(jnp.int32, (tk, tq), 0)
    qry = lax.broadcasted_iota(jnp.int32, (tk, tq), 1)
    n = jnp.maximum(delta + qry - key, 0)
    max_exact = REL_BUCKETS // 2
    large = max_exact + (jnp.log(jnp.maximum(n, 1).astype(F32) / max_exact)
                         / math.log(REL_MAX_DIST / max_exact) * (REL_BUCKETS - max_exact)).astype(jnp.int32)
    bucket = jnp.where(n < max_exact, n, jnp.minimum(large, REL_BUCKETS - 1))
    far = tab_ref[(REL_BUCKETS - 1) * n_heads + head]
    bias = jnp.zeros((tk, tq), F32)
    for j in range(REL_BUCKETS - 1):
        bias = jnp.where(bucket == j, tab_ref[j * n_heads + head] - far, bias)
    return bias * LOG2E


def _attn_body(tab_ref, lam_ref, qt_ref, k_ref, vt_ref, g_ref, sw_ref, o_ref, bias_ref, *, n_heads, seq):
    tq = tk = ATTN_BLOCK
    nq = seq // tq
    head = pl.program_id(0)

    @pl.when(pl.program_id(1) == 0)
    def _():
        for i, delta in enumerate((0, tq)):
            b = _rel_bias_tile(tab_ref, head, n_heads, delta, tk, tq)
            bias_ref[i] = jnp.concatenate([b, b], axis=1)

    lam = lam_ref[0]
    out_scale = lam_ref[1]
    feat = lax.broadcasted_iota(jnp.int32, (DA_V_DIM, tq), 0)
    first_map = feat < DA_QK_DIM
    key = lax.broadcasted_iota(jnp.int32, (tk, 2 * tq), 0)
    qry = lax.broadcasted_iota(jnp.int32, (tk, 2 * tq), 1)
    causal = jnp.where(qry >= tq, qry - tq, qry) >= key

    def scores(base, qst, k0, bias, masked):
        s = _dot(k_ref[pl.ds(base + k0, tk), :], qst)
        if bias is not None:
            s = s + bias
        if masked:
            s = jnp.where(causal, s, NEG_INF)
        return s

    def update(base, s, k0, carry):
        m, l, acc = carry
        m_new = jnp.maximum(m, jnp.max(s, axis=0, keepdims=True))
        alpha = jnp.exp2(m - m_new)
        p = jnp.exp2(s - m_new)
        v_ext = jnp.concatenate([vt_ref[:, pl.ds(base + k0, tk)], jnp.ones((ATTN_ONES_ROWS, tk), BF16)], axis=0)
        pv = _dot(v_ext, p.astype(BF16))
        l = alpha * l + pv[DA_V_DIM:DA_V_DIM + 1]
        acc = alpha * acc + pv[:DA_V_DIM]
        return m_new, l, acc

    def q_block(qi, q0):
        bases = [bi * seq for bi in range(ATTN_BATCH)]
        qsts = []
        for base in bases:
            qb = qt_ref[:, pl.ds(base + q0, tq)]
            zero = jnp.zeros_like(qb)
            qsts.append(jnp.concatenate([jnp.where(first_map, qb, zero), jnp.where(first_map, zero, qb)], axis=1))
        init = (jnp.full((1, 2 * tq), NEG_INF, F32), jnp.zeros((1, 2 * tq), F32),
                jnp.zeros((DA_V_DIM, 2 * tq), F32))

        def group(tiles, carries):
            ss = [[scores(base, qst, *tile) for base, qst in zip(bases, qsts)] for tile in tiles]
            for (k0, _, _), s_tile in zip(tiles, ss):
                carries = tuple(update(base, s, k0, cr) for base, s, cr in zip(bases, s_tile, carries))
            return carries

        far = lambda ki: (pl.multiple_of(ki * tk, tk), None, False)
        carries = (init,) * ATTN_BATCH
        if qi is None:
            carries = group([(q0, bias_ref[0], True)], carries)
        else:
            n_far = qi - 1
            carries = lax.fori_loop(
                0, n_far // 2, lambda kk, cr: group([far(2 * kk), far(2 * kk + 1)], cr), carries)
            last_two = [(pl.multiple_of(q0 - tk, tk), bias_ref[1], False), (q0, bias_ref[0], True)]
            carries = lax.cond(n_far % 2 == 1, lambda cr: group([far(n_far - 1)] + last_two, cr),
                               lambda cr: group(last_two, cr), carries)
        for base, (_, l, acc) in zip(bases, carries):
            o_t = acc[:, :tq] / l[:, :tq] - lam * (acc[:, tq:] / l[:, tq:])
            o = o_t.T
            o = o * _rms_scale(o) * sw_ref[...] * out_scale
            o_ref[pl.ds(base + q0, tq), :] = (o * _silu(g_ref[pl.ds(base + q0, tq), :])).astype(BF16)

    q_block(None, 0)

    def later_block(qi, _):
        q_block(qi, pl.multiple_of(qi * tq, tq))
        return 0

    lax.fori_loop(1, nq, later_block, 0)


def _diff_attn(qt, k, vt, g, table, lam_pair, sw, batch, seq):
    t, width = k.shape
    n_heads = width // DA_V_DIM
    blk = pl.BlockSpec((ATTN_BATCH * seq, DA_V_DIM), lambda h, b: (b, h))
    blk_t = pl.BlockSpec((DA_V_DIM, ATTN_BATCH * seq), lambda h, b: (h, b))
    smem = pl.BlockSpec(memory_space=pltpu.SMEM)
    return pl.pallas_call(
        functools.partial(_attn_body, n_heads=n_heads, seq=seq),
        grid=(n_heads, batch // ATTN_BATCH),
        in_specs=[smem, smem, blk_t, blk, blk_t, blk, _const_spec((1, DA_V_DIM))],
        out_specs=blk,
        out_shape=jax.ShapeDtypeStruct((t, width), BF16),
        scratch_shapes=[pltpu.VMEM((2, ATTN_BLOCK, 2 * ATTN_BLOCK), F32)],
        compiler_params=_params(("arbitrary", "arbitrary")),
        name="diff_attn",
    )(table.reshape(-1), lam_pair, qt, k, vt, g, sw[None, :])


def _hgrn_body(q_ref, k_ref, lf_ref, v_ref, g_ref, nw_ref, tri_ref, lvl_ref, y_ref, st_ref, b_ref):
    @pl.when(pl.program_id(2) == 0)
    def _():
        st_ref[...] = jnp.zeros(st_ref.shape, F32)

    c = q_ref.shape[0]
    dk = HG_HEAD_DIM
    heads = [slice(hi * dk, (hi + 1) * dk) for hi in range(HG_HEADS_PER_STEP)]
    lf_hi, lf_lo = _split_bf16(lf_ref[...], 2)
    b_ref[...] = _dot(tri_ref[...], jnp.concatenate([lf_hi, lf_lo], axis=0))

    outs = []
    for hi, hs in enumerate(heads):
        outs.append(_dot_nt((q_ref[:, hs] * jnp.exp2(b_ref[:, hs])).astype(BF16), st_ref[hi].astype(BF16)))
    for hi, hs in enumerate(heads):
        b_last = b_ref[c - 1:c, hs]
        kd = (k_ref[:, hs] * jnp.exp2(b_last - b_ref[:, hs])).astype(BF16)
        st_ref[hi] = st_ref[hi] * jnp.exp2(b_last) + _dot_tn(v_ref[:, hs], kd)

    row = lax.broadcasted_iota(jnp.int32, (c, dk), 0)
    lvl = lvl_ref[...]
    on_diag = lvl == 0
    atts = [jnp.where(on_diag, jnp.sum(q_ref[:, hs] * k_ref[:, hs], axis=-1, keepdims=True), 0.0)
            for hs in heads]
    m = c // 2
    while m >= 1:
        is_query = (row & m) != 0
        sign = jnp.where(is_query, 1.0, -1.0)
        zs = []
        for hs in heads:
            b = b_ref[:, hs]
            ref_rows = jnp.concatenate(
                [jnp.broadcast_to(b[u + m - 1:u + m, :], (2 * m, dk)) for u in range(0, c, 2 * m)], axis=0)
            zs.append((jnp.where(is_query, q_ref[:, hs], k_ref[:, hs])
                       * jnp.exp2((b - ref_rows) * sign)).astype(BF16))
        in_level = lvl == m
        atts = [jnp.where(in_level, _dot_nt(z, z), att) for z, att in zip(zs, atts)]
        m //= 2

    for hi, hs in enumerate(heads):
        o = outs[hi] + _dot(atts[hi].astype(BF16), v_ref[:, hs])
        o = o * _rms_scale(o) * nw_ref[...]
        y_ref[:, hs] = (o * _silu(g_ref[:, hs])).astype(BF16)


def _hgrn(q, k, lf, v, g, nw, batch, seq):
    t, width = q.shape
    c = HG_CHUNK
    nc = seq // c
    bw = HG_HEADS_PER_STEP * HG_HEAD_DIM
    idx = jnp.arange(c)
    tri = (idx[:, None] >= idx[None, :]).astype(BF16)
    tri2 = jnp.concatenate([tri, tri], axis=1)
    diff = idx[:, None] ^ idx[None, :]
    top_bit = jnp.left_shift(1, jnp.maximum(31 - lax.clz(diff), 0))
    lvl = jnp.where(idx[:, None] > idx[None, :], top_bit, jnp.where(diff == 0, 0, -1)).astype(jnp.int32)
    blk = pl.BlockSpec((c, bw), lambda b, hg, ci: (b * nc + ci, hg))
    return pl.pallas_call(
        _hgrn_body,
        grid=(batch, width // bw, nc),
        in_specs=[blk, blk, blk, blk, blk, _const_spec((1, HG_HEAD_DIM)), _const_spec((c, 2 * c)),
                  _const_spec((c, c))],
        out_specs=blk,
        out_shape=jax.ShapeDtypeStruct((t, width), BF16),
        scratch_shapes=[pltpu.VMEM((HG_HEADS_PER_STEP, HG_HEAD_DIM, HG_HEAD_DIM), F32),
                        pltpu.VMEM((c, bw), F32)],
        compiler_params=_params(("parallel", "parallel", "arbitrary")),
        name="hgrn",
    )(q, k, lf, v, g, nw[None, :], tri2, lvl)


def kernel(x, norm_w, final_norm_w, rel_bias, even_w_in, even_w_out, conv_w, conv_b, dt_bias, A_log, D_skip,
           ssd_norm_w, lambda_q1, lambda_k1, lambda_q2, lambda_k2, subln_w, odd_w_in, odd_w_out,
           hgrn_lower_bounds, hgrn_norm_w):
    batch, seq, d = x.shape
    depth = norm_w.shape[0]
    tm = PROJ_ROWS
    h = x.reshape(batch * seq, d)

    heads = A_log.shape[1]
    ssd_w = heads * SSD_HEAD_DIM
    conv_ch = conv_w.shape[2]
    da_w = rel_bias.shape[1] * DA_V_DIM
    offs = [0]
    for n in (ssd_w, conv_ch, heads, da_w, da_w, da_w, da_w):
        offs.append(offs[-1] + n)

    lb_all = jax.nn.softmax(hgrn_lower_bounds.astype(F32), axis=0)
    lb_all = jnp.cumsum(lb_all, axis=0) - lb_all[0]

    prev = None
    for layer in range(depth):
        nw = norm_w[layer][None, :]
        if layer % 2 == 0:
            e = layer // 2
            w = even_w_in[e]
            seg = lambda i: w[:, offs[i]:offs[i + 1]]
            w_perm = jnp.concatenate(
                [seg(0), seg(1), seg(4), seg(6), jnp.pad(seg(2), ((0, 0), (0, LANES - heads)))],
                axis=1).astype(BF16)
            w_t = jnp.concatenate([seg(3) * (DA_QK_DIM ** -0.5 * LOG2E), seg(5)], axis=1).T.astype(BF16)
            h, z, xbc, k, g, dt, q_t, v_t = _proj_even(h, nw, w_perm, w_t, tm, prev)
            y_a = _ssd(xbc, dt, z, conv_w[e], conv_b[e], dt_bias[e], -jnp.exp(A_log[e].astype(F32)) * LOG2E,
                       D_skip[e], ssd_norm_w[e], batch, seq)
            lam_init = 0.8 - 0.6 * math.exp(-0.3 * layer)
            lam = (jnp.exp(jnp.sum(lambda_q1[e].astype(F32) * lambda_k1[e].astype(F32)))
                   - jnp.exp(jnp.sum(lambda_q2[e].astype(F32) * lambda_k2[e].astype(F32))) + lam_init)
            lam_pair = jnp.stack([lam, jnp.asarray(1.0 - lam_init, F32)])
            y_b = _diff_attn(q_t, k, v_t, g, rel_bias.astype(F32), lam_pair, subln_w[e], batch, seq)
            w_out = even_w_out[e].astype(BF16)
            prev = ([y_a, y_b], [w_out[:ssd_w], w_out[ssd_w:]])
        else:
            o = layer // 2
            h, q, k, lf, v, g = _proj_odd(h, nw, odd_w_in[o].astype(BF16), lb_all[layer][None, :], tm // 2, prev)
            y = _hgrn(q, k, lf, v, g, hgrn_norm_w[o], batch, seq)
            prev = ([y], [odd_w_out[o].astype(BF16)])
    h = _out_proj(h, prev[0], prev[1], tm, final_norm_w[None, :])
    return h.reshape(batch, seq, d)
```

```python
import functools
import math

import jax
import jax.numpy as jnp
from jax import lax
from jax.experimental import pallas as pl
from jax.experimental.pallas import tpu as pltpu

F32 = jnp.float32
BF16 = jnp.bfloat16
EPS = 1e-6
NEG_INF = float("-inf")
LOG2E = math.log2(math.e)

LANES = 128
SSD_HEAD_DIM = 64
SSD_GROUPS = 4
SSD_STATE = 128
SSD_CONV = 4
SSD_CHUNK = 128
DA_QK_DIM = 64
DA_V_DIM = 128
REL_BUCKETS = 32
REL_MAX_DIST = 128
HG_HEAD_DIM = 128
HG_CHUNK = 128
HG_HEADS_PER_STEP = 16
ATTN_BLOCK = 256
ATTN_BATCH = 4
ATTN_ONES_ROWS = 16
PROJ_ROWS = 512
VMEM_LIMIT = 56 * 1024 * 1024


def _dot(a, b):
    return jnp.dot(a, b, preferred_element_type=F32)


def _dot_nt(a, b):
    return lax.dot_general(a, b, (((1,), (1,)), ((), ())), preferred_element_type=F32)


def _dot_tn(a, b):
    return lax.dot_general(a, b, (((0,), (0,)), ((), ())), preferred_element_type=F32)


def _split_bf16(x, parts):
    out = []
    r = x
    for _ in range(parts):
        p = r.astype(BF16)
        out.append(p)
        r = r - p.astype(F32)
    return out


def _dot_exact_rhs(x, m_bf16, parts):
    return sum(_dot(p, m_bf16) for p in _split_bf16(x, parts))


def _dot_exact_lhs(m_bf16, x, parts):
    return sum(_dot(m_bf16, p) for p in _split_bf16(x, parts))


def _sigmoid(x):
    return 1.0 / (1.0 + jnp.exp(-x))


def _silu(x):
    return x * _sigmoid(x)


def _rms_scale(x):
    return lax.rsqrt(jnp.mean(x * x, axis=-1, keepdims=True) + EPS)


def _const_spec(shape):
    nd = len(shape)
    return pl.BlockSpec(shape, lambda *_: (0,) * nd, pipeline_mode=pl.Buffered(1))


def _params(sem):
    return pltpu.CompilerParams(dimension_semantics=sem, vmem_limit_bytes=VMEM_LIMIT)


def _residual_rows(h_ref, act_refs, wo_ref):
    x = h_ref[...]
    row = 0
    for a_ref in act_refs:
        x = x + _dot(a_ref[...], wo_ref[row:row + a_ref.shape[1], :])
        row += a_ref.shape[1]
    return x


def _block_input(h_ref, act_refs, wo_ref, hn_ref):
    if not act_refs:
        return h_ref[...]
    x = _residual_rows(h_ref, act_refs, wo_ref)
    hn_ref[...] = x
    return x


def _split_refs(refs, n_act, n_fixed):
    n_prev = n_act + 1 if n_act else 0
    acts = refs[1:1 + n_act]
    wo_ref = refs[n_act + 1] if n_act else None
    fixed = refs[1 + n_prev:1 + n_prev + n_fixed]
    outs = refs[1 + n_prev + n_fixed:]
    hn_ref = None
    if n_act:
        hn_ref, outs = outs[0], outs[1:]
    return refs[0], acts, wo_ref, fixed, hn_ref, outs


def _prev_specs(prev, tm):
    if prev is None:
        return [], []
    acts, w_out = prev
    specs = [pl.BlockSpec((tm, a.shape[1]), lambda i: (i, 0)) for a in acts] + [_const_spec(w_out.shape)]
    return list(acts) + [w_out], specs


def _proj_even_body(*refs, n_act):
    h_ref, acts, wo_ref, (nw_ref, w_ref, wt_ref), hn_ref, outs = _split_refs(refs, n_act, 3)
    z_ref, xbc_ref, k_ref, g_ref, dt_ref, qt_ref, vt_ref = outs
    x = _block_input(h_ref, acts, wo_ref, hn_ref)
    u = (x * _rms_scale(x) * nw_ref[...]).astype(BF16)
    col = 0
    for ref in (z_ref, xbc_ref, k_ref, g_ref, dt_ref):
        n = ref.shape[1]
        step = min(n, 512)
        for j in range(0, n, step):
            ref[:, j:j + step] = _dot(u, w_ref[:, col + j:col + j + step]).astype(ref.dtype)
        col += n
    row = 0
    for ref in (qt_ref, vt_ref):
        n = ref.shape[0]
        for j in range(0, n, 256):
            ref[j:j + 256, :] = _dot_nt(wt_ref[row + j:row + j + 256, :], u).astype(ref.dtype)
        row += n


def _proj_even(h, nw, w, wt, tm, prev=None):
    t, d = h.shape
    widths = (d, 2 * d, d, d, LANES)
    dtypes = (F32, F32, BF16, F32, F32)
    row = lambda n: pl.BlockSpec((tm, n), lambda i: (i, 0))
    col = pl.BlockSpec((d, tm), lambda i: (0, i))
    prev_args, prev_specs = _prev_specs(prev, tm)
    out_specs = [row(n) for n in widths] + [col, col]
    out_shape = ([jax.ShapeDtypeStruct((t, n), dt) for n, dt in zip(widths, dtypes)]
                 + [jax.ShapeDtypeStruct((d, t), BF16)] * 2)
    if prev is not None:
        out_specs, out_shape = [row(d)] + out_specs, [jax.ShapeDtypeStruct((t, d), F32)] + out_shape
    outs = pl.pallas_call(
        functools.partial(_proj_even_body, n_act=max(len(prev_args) - 1, 0)),
        grid=(t // tm,),
        in_specs=[row(d)] + prev_specs + [_const_spec((1, d)), _const_spec(w.shape), _const_spec(wt.shape)],
        out_specs=out_specs,
        out_shape=out_shape,
        compiler_params=_params(("parallel",)),
        name="proj_even",
    )(h, *prev_args, nw, w, wt)
    return tuple(outs) if prev is not None else (h, *outs)


def _proj_odd_body(*refs, n_act):
    h_ref, acts, wo_ref, (nw_ref, w_ref, lb_ref), hn_ref, outs = _split_refs(refs, n_act, 3)
    q_ref, k_ref, lf_ref, v_ref, g_ref = outs
    x = _block_input(h_ref, acts, wo_ref, hn_ref)
    u = (x * _rms_scale(x) * nw_ref[...]).astype(BF16)
    n = q_ref.shape[1]
    step = 512
    for j in range(0, n, step):
        cs = slice(j, j + step)
        q_ref[:, cs] = _silu(_dot(u, w_ref[:, j:j + step]))
        lb = lb_ref[:, cs]
        f = lb + (1.0 - lb) * _sigmoid(_dot(u, w_ref[:, n + j:n + j + step]))
        lf_ref[:, cs] = jnp.log(f) * LOG2E
        k_ref[:, cs] = 1.0 - f
        v_ref[:, cs] = _dot(u, w_ref[:, 2 * n + j:2 * n + j + step]).astype(BF16)
        g_ref[:, cs] = _dot(u, w_ref[:, 3 * n + j:3 * n + j + step])


def _proj_odd(h, nw, w, lb, tm, prev=None):
    t, d = h.shape
    n = w.shape[1] // 4
    dtypes = (F32, F32, F32, BF16, F32)
    row = lambda m: pl.BlockSpec((tm, m), lambda i: (i, 0))
    prev_args, prev_specs = _prev_specs(prev, tm)
    out_specs = [row(n) for _ in dtypes]
    out_shape = [jax.ShapeDtypeStruct((t, n), dt) for dt in dtypes]
    if prev is not None:
        out_specs, out_shape = [row(d)] + out_specs, [jax.ShapeDtypeStruct((t, d), F32)] + out_shape
    outs = pl.pallas_call(
        functools.partial(_proj_odd_body, n_act=max(len(prev_args) - 1, 0)),
        grid=(t // tm,),
        in_specs=[row(d)] + prev_specs + [_const_spec((1, d)), _const_spec(w.shape), _const_spec((1, n))],
        out_specs=out_specs,
        out_shape=out_shape,
        compiler_params=_params(("parallel",)),
        name="proj_odd",
    )(h, *prev_args, nw, w, lb)
    return tuple(outs) if prev is not None else (h, *outs)


def _out_proj_body(*refs, n_act):
    h_ref, acts, wo_ref, (fw_ref,), o_ref, _ = _split_refs(refs, n_act, 1)
    x = _residual_rows(h_ref, acts, wo_ref)
    o_ref[...] = x * _rms_scale(x) * fw_ref[...]


def _out_proj(h, acts, w_out, tm, final_w):
    t, d = h.shape
    row = lambda n: pl.BlockSpec((tm, n), lambda i: (i, 0))
    prev_args, prev_specs = _prev_specs((acts, w_out), tm)
    return pl.pallas_call(
        functools.partial(_out_proj_body, n_act=len(acts)),
        grid=(t // tm,),
        in_specs=[row(d)] + prev_specs + [_const_spec((1, d))],
        out_specs=row(d),
        out_shape=jax.ShapeDtypeStruct((t, d), F32),
        compiler_params=_params(("parallel",)),
        name="out_proj",
    )(h, *prev_args, final_w)


def _softplus(x):
    return jnp.maximum(x, 0.0) + jnp.log(1.0 + jnp.exp(-jnp.abs(x)))


def _ssd_body(xbc_ref, dt_ref, z_ref, cw_ref, cb_ref, dtb_ref, a_ref, dsk_ref, nw_ref, tri_ref,
              exp_ref, y_ref, ext_ref, st_ref, yacc_ref, *, width):
    cs = SSD_CHUNK
    gw = width // SSD_GROUPS
    hpg = gw // SSD_HEAD_DIM
    pad = 8
    c = pl.program_id(1)

    @pl.when(c == 0)
    def _():
        ext_ref[0:pad, :] = jnp.zeros((pad, ext_ref.shape[1]), F32)
        st_ref[...] = jnp.zeros(st_ref.shape, F32)

    xin = xbc_ref[...]
    ext_ref[pad:pad + cs, :] = xin
    conv = cb_ref[...] + cw_ref[SSD_CONV - 1:SSD_CONV, :] * xin
    for j in range(1, SSD_CONV):
        conv = conv + cw_ref[SSD_CONV - 1 - j:SSD_CONV - j, :] * ext_ref[pad - j:pad - j + cs, :]
    ext_ref[0:pad, :] = xin[cs - pad:cs, :]
    act = _silu(conv)

    dt = _softplus(dt_ref[...] + dtb_ref[...])
    a = dt * a_ref[...]
    a_cs = _dot_exact_lhs(tri_ref[...], a, 3)
    a_cs_t = a_cs.T
    dt_t = dt.T
    a_last = a_cs[cs - 1:cs, :]
    per_head = jnp.concatenate(
        [dt * jnp.exp2(a_last - a_cs), jnp.exp2(a_cs), jnp.broadcast_to(jnp.exp2(a_last), (8, LANES))], axis=0)
    per_col = _dot_exact_rhs(per_head, exp_ref[...], 2)
    w_state = per_col[0:cs]
    e_acs = per_col[cs:2 * cs]
    e_last = per_col[2 * cs:2 * cs + 1]

    row = lax.broadcasted_iota(jnp.int32, (cs, cs), 0)
    colm = lax.broadcasted_iota(jnp.int32, (cs, cs), 1)
    tril = row >= colm
    lane = lax.broadcasted_iota(jnp.int32, (cs, LANES), 1)
    low_half = lane < SSD_HEAD_DIM

    group_cols = [slice(g * gw, (g + 1) * gw) for g in range(SSD_GROUPS)]
    cbs, y_offs = [], []
    for g, gs in enumerate(group_cols):
        bg = act[:, width + g * SSD_STATE:width + (g + 1) * SSD_STATE].astype(BF16)
        cg = act[:, width + (SSD_GROUPS + g) * SSD_STATE:width + (SSD_GROUPS + g + 1) * SSD_STATE].astype(BF16)
        cbs.append(_dot_nt(cg, bg))
        s_prev = st_ref[g]
        y_offs.append(_dot(cg, s_prev.astype(BF16)))
        xd = (act[:, gs] * w_state[:, gs]).astype(BF16)
        st_ref[g] = s_prev * e_last[:, gs] + _dot_tn(bg, xd)
    for g, gs in enumerate(group_cols):
        xg = act[:, gs]
        y_off = y_offs[g] * e_acs[:, gs]
        for p in range(gw // LANES):
            xp = xg[:, p * LANES:(p + 1) * LANES]
            halves = (jnp.where(low_half, xp, 0.0).astype(BF16), jnp.where(low_half, 0.0, xp).astype(BF16))
            acc = y_off[:, p * LANES:(p + 1) * LANES] + xp * dsk_ref[:, g * gw + p * LANES:g * gw + (p + 1) * LANES]
            for r in range(LANES // SSD_HEAD_DIM):
                hd = g * hpg + p * (LANES // SSD_HEAD_DIM) + r
                decay = jnp.exp2(jnp.where(tril, a_cs[:, hd:hd + 1] - a_cs_t[hd:hd + 1, :], NEG_INF))
                scores = (cbs[g] * decay * dt_t[hd:hd + 1, :]).astype(BF16)
                acc = acc + _dot(scores, halves[r])
            yacc_ref[:, g * gw + p * LANES:g * gw + (p + 1) * LANES] = acc

    yz = yacc_ref[...] * _silu(z_ref[...])
    y_ref[...] = (yz * _rms_scale(yz) * nw_ref[...]).astype(BF16)


def _ssd(xbc, dt, z, cw, cb, dtb, a_neg, dsk, nw, batch, seq):
    t, width = z.shape
    cs = SSD_CHUNK
    nc = seq // cs
    heads = width // SSD_HEAD_DIM
    tri = (jnp.arange(cs)[:, None] >= jnp.arange(cs)[None, :]).astype(BF16)
    expand = (jnp.arange(LANES)[:, None] == (jnp.arange(width)[None, :] // SSD_HEAD_DIM)).astype(BF16)
    pad_h = lambda v: jnp.pad(v.astype(F32), (0, LANES - heads))[None, :]
    blk = lambda n: pl.BlockSpec((cs, n), lambda b, c: (b * nc + c, 0))
    return pl.pallas_call(
        functools.partial(_ssd_body, width=width),
        grid=(batch, nc),
        in_specs=[blk(xbc.shape[1]), blk(LANES), blk(width),
                  _const_spec(cw.shape), _const_spec((1, xbc.shape[1])), _const_spec((1, LANES)),
                  _const_spec((1, LANES)), _const_spec((1, width)), _const_spec((1, width)),
                  _const_spec((cs, cs)), _const_spec((LANES, width))],
        out_specs=blk(width),
        out_shape=jax.ShapeDtypeStruct((t, width), BF16),
        scratch_shapes=[pltpu.VMEM((cs + 8, xbc.shape[1]), F32),
                        pltpu.VMEM((SSD_GROUPS, SSD_STATE, width // SSD_GROUPS), F32),
                        pltpu.VMEM((cs, width), F32)],
        compiler_params=_params(("parallel", "arbitrary")),
        name="ssd",
    )(xbc, dt, z, cw, cb[None, :], pad_h(dtb), pad_h(a_neg),
      jnp.repeat(dsk.astype(F32), SSD_HEAD_DIM)[None, :], nw[None, :], tri, expand)


def _rel_bias_tile(tab_ref, head, n_heads, delta, tk, tq):
    key = lax.broadcasted_iota(jnp.int32, (tk, tq), 0)
    qry = lax.broadcasted_iota(jnp.int32, (tk, tq), 1)
    n = jnp.maximum(delta + qry - key, 0)
    max_exact = REL_BUCKETS // 2
    large = max_exact + (jnp.log(jnp.maximum(n, 1).astype(F32) / max_exact)
                         / math.log(REL_MAX_DIST / max_exact) * (REL_BUCKETS - max_exact)).astype(jnp.int32)
    bucket = jnp.where(n < max_exact, n, jnp.minimum(large, REL_BUCKETS - 1))
    far = tab_ref[(REL_BUCKETS - 1) * n_heads + head]
    bias = jnp.zeros((tk, tq), F32)
    for j in range(REL_BUCKETS - 1):
        bias = jnp.where(bucket == j, tab_ref[j * n_heads + head] - far, bias)
    return bias * LOG2E


def _attn_body(tab_ref, lam_ref, qt_ref, k_ref, vt_ref, g_ref, sw_ref, o_ref, bias_ref, *, n_heads, seq):
    tq = tk = ATTN_BLOCK
    nq = seq // tq
    head = pl.program_id(0)

    @pl.when(pl.program_id(1) == 0)
    def _():
        for i, delta in enumerate((0, tq)):
            b = _rel_bias_tile(tab_ref, head, n_heads, delta, tk, tq)
            bias_ref[i] = jnp.concatenate([b, b], axis=1)

    lam = lam_ref[0]
    out_scale = lam_ref[1]
    feat = lax.broadcasted_iota(jnp.int32, (DA_V_DIM, tq), 0)
    first_map = feat < DA_QK_DIM
    key = lax.broadcasted_iota(jnp.int32, (tk, 2 * tq), 0)
    qry = lax.broadcasted_iota(jnp.int32, (tk, 2 * tq), 1)
    causal = jnp.where(qry >= tq, qry - tq, qry) >= key

    def scores(base, qst, k0, bias, masked):
        s = _dot(k_ref[pl.ds(base + k0, tk), :], qst)
        if bias is not None:
            s = s + bias
        if masked:
            s = jnp.where(causal, s, NEG_INF)
        return s

    def update(base, s, k0, carry):
        m, l, acc = carry
        m_new = jnp.maximum(m, jnp.max(s, axis=0, keepdims=True))
        alpha = jnp.exp2(m - m_new)
        p = jnp.exp2(s - m_new)
        v_ext = jnp.concatenate([vt_ref[:, pl.ds(base + k0, tk)], jnp.ones((ATTN_ONES_ROWS, tk), BF16)], axis=0)
        pv = _dot(v_ext, p.astype(BF16))
        l = alpha * l + pv[DA_V_DIM:DA_V_DIM + 1]
        acc = alpha * acc + pv[:DA_V_DIM]
        return m_new, l, acc

    def q_block(qi, q0):
        bases = [bi * seq for bi in range(ATTN_BATCH)]
        qsts = []
        for base in bases:
            qb = qt_ref[:, pl.ds(base + q0, tq)]
            zero = jnp.zeros_like(qb)
            qsts.append(jnp.concatenate([jnp.where(first_map, qb, zero), jnp.where(first_map, zero, qb)], axis=1))
        init = (jnp.full((1, 2 * tq), NEG_INF, F32), jnp.zeros((1, 2 * tq), F32),
                jnp.zeros((DA_V_DIM, 2 * tq), F32))

        def group(tiles, carries):
            ss = [[scores(base, qst, *tile) for base, qst in zip(bases, qsts)] for tile in tiles]
            for (k0, _, _), s_tile in zip(tiles, ss):
                carries = tuple(update(base, s, k0, cr) for base, s, cr in zip(bases, s_tile, carries))
            return carries

        far = lambda ki: (pl.multiple_of(ki * tk, tk), None, False)
        carries = (init,) * ATTN_BATCH
        if qi is None:
            carries = group([(q0, bias_ref[0], True)], carries)
        else:
            n_far = qi - 1
            carries = lax.fori_loop(
                0, n_far // 2, lambda kk, cr: group([far(2 * kk), far(2 * kk + 1)], cr), carries)
            last_two = [(pl.multiple_of(q0 - tk, tk), bias_ref[1], False), (q0, bias_ref[0], True)]
            carries = lax.cond(n_far % 2 == 1, lambda cr: group([far(n_far - 1)] + last_two, cr),
                               lambda cr: group(last_two, cr), carries)
        for base, (_, l, acc) in zip(bases, carries):
            o_t = acc[:, :tq] / l[:, :tq] - lam * (acc[:, tq:] / l[:, tq:])
            o = o_t.T
            o = o * _rms_scale(o) * sw_ref[...] * out_scale
            o_ref[pl.ds(base + q0, tq), :] = (o * _silu(g_ref[pl.ds(base + q0, tq), :])).astype(BF16)

    q_block(None, 0)

    def later_block(qi, _):
        q_block(qi, pl.multiple_of(qi * tq, tq))
        return 0

    lax.fori_loop(1, nq, later_block, 0)


def _diff_attn(qt, k, vt, g, table, lam_pair, sw, batch, seq):
    t, width = k.shape
    n_heads = width // DA_V_DIM
    blk = pl.BlockSpec((ATTN_BATCH * seq, DA_V_DIM), lambda h, b: (b, h))
    blk_t = pl.BlockSpec((DA_V_DIM, ATTN_BATCH * seq), lambda h, b: (h, b))
    smem = pl.BlockSpec(memory_space=pltpu.SMEM)
    return pl.pallas_call(
        functools.partial(_attn_body, n_heads=n_heads, seq=seq),
        grid=(n_heads, batch // ATTN_BATCH),
        in_specs=[smem, smem, blk_t, blk, blk_t, blk, _const_spec((1, DA_V_DIM))],
        out_specs=blk,
        out_shape=jax.ShapeDtypeStruct((t, width), BF16),
        scratch_shapes=[pltpu.VMEM((2, ATTN_BLOCK, 2 * ATTN_BLOCK), F32)],
        compiler_params=_params(("arbitrary", "arbitrary")),
        name="diff_attn",
    )(table.reshape(-1), lam_pair, qt, k, vt, g, sw[None, :])


def _hgrn_body(q_ref, k_ref, lf_ref, v_ref, g_ref, nw_ref, tri_ref, lvl_ref, y_ref, st_ref, b_ref):
    @pl.when(pl.program_id(2) == 0)
    def _():
        st_ref[...] = jnp.zeros(st_ref.shape, F32)

    c = q_ref.shape[0]
    dk = HG_HEAD_DIM
    heads = [slice(hi * dk, (hi + 1) * dk) for hi in range(HG_HEADS_PER_STEP)]
    lf_hi, lf_lo = _split_bf16(lf_ref[...], 2)
    b_ref[...] = _dot(tri_ref[...], jnp.concatenate([lf_hi, lf_lo], axis=0))

    outs = []
    for hi, hs in enumerate(heads):
        outs.append(_dot_nt((q_ref[:, hs] * jnp.exp2(b_ref[:, hs])).astype(BF16), st_ref[hi].astype(BF16)))
    for hi, hs in enumerate(heads):
        b_last = b_ref[c - 1:c, hs]
        kd = (k_ref[:, hs] * jnp.exp2(b_last - b_ref[:, hs])).astype(BF16)
        st_ref[hi] = st_ref[hi] * jnp.exp2(b_last) + _dot_tn(v_ref[:, hs], kd)

    row = lax.broadcasted_iota(jnp.int32, (c, dk), 0)
    lvl = lvl_ref[...]
    on_diag = lvl == 0
    atts = [jnp.where(on_diag, jnp.sum(q_ref[:, hs] * k_ref[:, hs], axis=-1, keepdims=True), 0.0)
            for hs in heads]
    m = c // 2
    while m >= 1:
        is_query = (row & m) != 0
        sign = jnp.where(is_query, 1.0, -1.0)
        zs = []
        for hs in heads:
            b = b_ref[:, hs]
            ref_rows = jnp.concatenate(
                [jnp.broadcast_to(b[u + m - 1:u + m, :], (2 * m, dk)) for u in range(0, c, 2 * m)], axis=0)
            zs.append((jnp.where(is_query, q_ref[:, hs], k_ref[:, hs])
                       * jnp.exp2((b - ref_rows) * sign)).astype(BF16))
        in_level = lvl == m
        atts = [jnp.where(in_level, _dot_nt(z, z), att) for z, att in zip(zs, atts)]
        m //= 2

    for hi, hs in enumerate(heads):
        o = outs[hi] + _dot(atts[hi].astype(BF16), v_ref[:, hs])
        o = o * _rms_scale(o) * nw_ref[...]
        y_ref[:, hs] = (o * _silu(g_ref[:, hs])).astype(BF16)


def _hgrn(q, k, lf, v, g, nw, batch, seq):
    t, width = q.shape
    c = HG_CHUNK
    nc = seq // c
    bw = HG_HEADS_PER_STEP * HG_HEAD_DIM
    idx = jnp.arange(c)
    tri = (idx[:, None] >= idx[None, :]).astype(BF16)
    tri2 = jnp.concatenate([tri, tri], axis=1)
    diff = idx[:, None] ^ idx[None, :]
    top_bit = jnp.left_shift(1, jnp.maximum(31 - lax.clz(diff), 0))
    lvl = jnp.where(idx[:, None] > idx[None, :], top_bit, jnp.where(diff == 0, 0, -1)).astype(jnp.int32)
    blk = pl.BlockSpec((c, bw), lambda b, hg, ci: (b * nc + ci, hg))
    return pl.pallas_call(
        _hgrn_body,
        grid=(batch, width // bw, nc),
        in_specs=[blk, blk, blk, blk, blk, _const_spec((1, HG_HEAD_DIM)), _const_spec((c, 2 * c)),
                  _const_spec((c, c))],
        out_specs=blk,
        out_shape=jax.ShapeDtypeStruct((t, width), BF16),
        scratch_shapes=[pltpu.VMEM((HG_HEADS_PER_STEP, HG_HEAD_DIM, HG_HEAD_DIM), F32),
                        pltpu.VMEM((c, bw), F32)],
        compiler_params=_params(("parallel", "parallel", "arbitrary")),
        name="hgrn",
    )(q, k, lf, v, g, nw[None, :], tri2, lvl)


def kernel(x, norm_w, final_norm_w, rel_bias, even_w_in, even_w_out, conv_w, conv_b, dt_bias, A_log, D_skip,
           ssd_norm_w, lambda_q1, lambda_k1, lambda_q2, lambda_k2, subln_w, odd_w_in, odd_w_out,
           hgrn_lower_bounds, hgrn_norm_w):
    batch, seq, d = x.shape
    depth = norm_w.shape[0]
    tm = PROJ_ROWS
    h = x.reshape(batch * seq, d)

    heads = A_log.shape[1]
    ssd_w = heads * SSD_HEAD_DIM
    conv_ch = conv_w.shape[2]
    da_w = rel_bias.shape[1] * DA_V_DIM
    offs = [0]
    for n in (ssd_w, conv_ch, heads, da_w, da_w, da_w, da_w):
        offs.append(offs[-1] + n)

    lb_all = jax.nn.softmax(hgrn_lower_bounds.astype(F32), axis=0)
    lb_all = jnp.cumsum(lb_all, axis=0) - lb_all[0]

    prev = None
    for layer in range(depth):
        nw = norm_w[layer][None, :]
        if layer % 2 == 0:
            e = layer // 2
            w = even_w_in[e]
            seg = lambda i: w[:, offs[i]:offs[i + 1]]
            w_perm = jnp.concatenate(
                [seg(0), seg(1), seg(4), seg(6), jnp.pad(seg(2), ((0, 0), (0, LANES - heads)))],
                axis=1).astype(BF16)
            w_t = jnp.concatenate([seg(3) * (DA_QK_DIM ** -0.5 * LOG2E), seg(5)], axis=1).T.astype(BF16)
            h, z, xbc, k, g, dt, q_t, v_t = _proj_even(h, nw, w_perm, w_t, tm, prev)
            y_a = _ssd(xbc, dt, z, conv_w[e], conv_b[e], dt_bias[e], -jnp.exp(A_log[e].astype(F32)) * LOG2E,
                       D_skip[e], ssd_norm_w[e], batch, seq)
            lam_init = 0.8 - 0.6 * math.exp(-0.3 * layer)
            lam = (jnp.exp(jnp.sum(lambda_q1[e].astype(F32) * lambda_k1[e].astype(F32)))
                   - jnp.exp(jnp.sum(lambda_q2[e].astype(F32) * lambda_k2[e].astype(F32))) + lam_init)
            lam_pair = jnp.stack([lam, jnp.asarray(1.0 - lam_init, F32)])
            y_b = _diff_attn(q_t, k, v_t, g, rel_bias.astype(F32), lam_pair, subln_w[e], batch, seq)
            prev = ([y_a, y_b], even_w_out[e].astype(BF16))
        else:
            o = layer // 2
            h, q, k, lf, v, g = _proj_odd(h, nw, odd_w_in[o].astype(BF16), lb_all[layer][None, :], tm // 2, prev)
            y = _hgrn(q, k, lf, v, g, hgrn_norm_w[o], batch, seq)
            prev = ([y], odd_w_out[o].astype(BF16))
    h = _out_proj(h, prev[0], prev[1], tm, final_norm_w[None, :])
    return h.reshape(batch, seq, d)
```

```python
import functools
import math

import jax
import jax.numpy as jnp
from jax import lax
from jax.experimental import pallas as pl
from jax.experimental.pallas import tpu as pltpu

F32 = jnp.float32
BF16 = jnp.bfloat16
EPS = 1e-6
NEG_INF = float("-inf")
LOG2E = math.log2(math.e)

LANES = 128
SSD_HEAD_DIM = 64
SSD_GROUPS = 4
SSD_STATE = 128
SSD_CONV = 4
SSD_CHUNK = 128
DA_QK_DIM = 64
DA_V_DIM = 128
REL_BUCKETS = 32
REL_MAX_DIST = 128
HG_HEAD_DIM = 128
HG_CHUNK = 128
HG_HEADS_PER_STEP = 16
ATTN_BLOCK = 256
ATTN_BATCH = 4
ATTN_ONES_ROWS = 16
PROJ_ROWS = 512
VMEM_LIMIT = 56 * 1024 * 1024


def _dot(a, b):
    return jnp.dot(a, b, preferred_element_type=F32)


def _dot_nt(a, b):
    return lax.dot_general(a, b, (((1,), (1,)), ((), ())), preferred_element_type=F32)


def _dot_tn(a, b):
    return lax.dot_general(a, b, (((0,), (0,)), ((), ())), preferred_element_type=F32)


def _split_bf16(x, parts):
    out = []
    r = x
    for _ in range(parts):
        p = r.astype(BF16)
        out.append(p)
        r = r - p.astype(F32)
    return out


def _dot_exact_rhs(x, m_bf16, parts):
    return sum(_dot(p, m_bf16) for p in _split_bf16(x, parts))


def _dot_exact_lhs(m_bf16, x, parts):
    return sum(_dot(m_bf16, p) for p in _split_bf16(x, parts))


def _sigmoid(x):
    return 1.0 / (1.0 + jnp.exp(-x))


def _silu(x):
    return x * _sigmoid(x)


def _rms_scale(x):
    return lax.rsqrt(jnp.mean(x * x, axis=-1, keepdims=True) + EPS)


def _const_spec(shape):
    nd = len(shape)
    return pl.BlockSpec(shape, lambda *_: (0,) * nd, pipeline_mode=pl.Buffered(1))


def _layer_spec(stacked, layer):
    return pl.BlockSpec((None,) + stacked.shape[1:], lambda *_: (layer, 0, 0), pipeline_mode=pl.Buffered(1))


def _params(sem):
    return pltpu.CompilerParams(dimension_semantics=sem, vmem_limit_bytes=VMEM_LIMIT)


def _transpose_body(x_ref, o_ref):
    o_ref[...] = x_ref[...].T.astype(o_ref.dtype)


def _transposed_bf16(w, tile=256):
    rows, cols = w.shape
    return pl.pallas_call(
        _transpose_body,
        grid=(cols // tile, rows // tile),
        in_specs=[pl.BlockSpec((tile, tile), lambda i, j: (j, i))],
        out_specs=pl.BlockSpec((tile, tile), lambda i, j: (i, j)),
        out_shape=jax.ShapeDtypeStruct((cols, rows), BF16),
        compiler_params=_params(("parallel", "parallel")),
        name="weight_transpose",
    )(w)


def _residual_rows(h_ref, act_refs, wo_ref):
    x = h_ref[...]
    row = 0
    for a_ref in act_refs:
        x = x + _dot(a_ref[...], wo_ref[row:row + a_ref.shape[1], :])
        row += a_ref.shape[1]
    return x


def _block_input(h_ref, act_refs, wo_ref, hn_ref):
    if not act_refs:
        return h_ref[...]
    x = _residual_rows(h_ref, act_refs, wo_ref)
    hn_ref[...] = x
    return x


def _split_refs(refs, n_act, n_fixed):
    n_prev = n_act + 1 if n_act else 0
    acts = refs[1:1 + n_act]
    wo_ref = refs[n_act + 1] if n_act else None
    fixed = refs[1 + n_prev:1 + n_prev + n_fixed]
    outs = refs[1 + n_prev + n_fixed:]
    hn_ref = None
    if n_act:
        hn_ref, outs = outs[0], outs[1:]
    return refs[0], acts, wo_ref, fixed, hn_ref, outs


def _prev_specs(prev, tm):
    if prev is None:
        return [], []
    acts, w_out, layer = prev
    specs = [pl.BlockSpec((tm, a.shape[1]), lambda i: (i, 0)) for a in acts] + [_layer_spec(w_out, layer)]
    return list(acts) + [w_out], specs


def _proj_even_body(*refs, n_act):
    h_ref, acts, wo_ref, (nw_ref, w_ref, wt_ref), hn_ref, outs = _split_refs(refs, n_act, 3)
    z_ref, xbc_ref, k_ref, g_ref, dt_ref, qt_ref, vt_ref = outs
    x = _block_input(h_ref, acts, wo_ref, hn_ref)
    u = (x * _rms_scale(x) * nw_ref[...]).astype(BF16)
    col = 0
    for ref in (z_ref, xbc_ref, k_ref, g_ref, dt_ref):
        n = ref.shape[1]
        step = min(n, 512)
        for j in range(0, n, step):
            ref[:, j:j + step] = _dot(u, w_ref[:, col + j:col + j + step]).astype(ref.dtype)
        col += n
    row = 0
    for ref in (qt_ref, vt_ref):
        n = ref.shape[0]
        for j in range(0, n, 256):
            ref[j:j + 256, :] = _dot_nt(wt_ref[row + j:row + j + 256, :], u).astype(ref.dtype)
        row += n


def _proj_even(h, nw, w, wt, tm, prev=None):
    t, d = h.shape
    widths = (d, 2 * d, d, d, LANES)
    dtypes = (F32, F32, BF16, F32, F32)
    row = lambda n: pl.BlockSpec((tm, n), lambda i: (i, 0))
    col = pl.BlockSpec((d, tm), lambda i: (0, i))
    prev_args, prev_specs = _prev_specs(prev, tm)
    out_specs = [row(n) for n in widths] + [col, col]
    out_shape = ([jax.ShapeDtypeStruct((t, n), dt) for n, dt in zip(widths, dtypes)]
                 + [jax.ShapeDtypeStruct((d, t), BF16)] * 2)
    if prev is not None:
        out_specs, out_shape = [row(d)] + out_specs, [jax.ShapeDtypeStruct((t, d), F32)] + out_shape
    outs = pl.pallas_call(
        functools.partial(_proj_even_body, n_act=max(len(prev_args) - 1, 0)),
        grid=(t // tm,),
        in_specs=[row(d)] + prev_specs + [_const_spec((1, d)), _const_spec(w.shape), _const_spec(wt.shape)],
        out_specs=out_specs,
        out_shape=out_shape,
        compiler_params=_params(("parallel",)),
        name="proj_even",
    )(h, *prev_args, nw, w, wt)
    return tuple(outs) if prev is not None else (h, *outs)


def _proj_odd_body(*refs, n_act):
    h_ref, acts, wo_ref, (nw_ref, w_ref, lb_ref), hn_ref, outs = _split_refs(refs, n_act, 3)
    q_ref, k_ref, lf_ref, v_ref, g_ref = outs
    x = _block_input(h_ref, acts, wo_ref, hn_ref)
    u = (x * _rms_scale(x) * nw_ref[...]).astype(BF16)
    n = q_ref.shape[1]
    step = 512
    for j in range(0, n, step):
        cs = slice(j, j + step)
        q_ref[:, cs] = _silu(_dot(u, w_ref[:, j:j + step]))
        lb = lb_ref[:, cs]
        f = lb + (1.0 - lb) * _sigmoid(_dot(u, w_ref[:, n + j:n + j + step]))
        lf_ref[:, cs] = jnp.log(f) * LOG2E
        k_ref[:, cs] = 1.0 - f
        v_ref[:, cs] = _dot(u, w_ref[:, 2 * n + j:2 * n + j + step]).astype(BF16)
        g_ref[:, cs] = _dot(u, w_ref[:, 3 * n + j:3 * n + j + step])


def _proj_odd(h, nw, w, layer, lb, tm, prev=None):
    t, d = h.shape
    n = w.shape[2] // 4
    dtypes = (F32, F32, F32, BF16, F32)
    row = lambda m: pl.BlockSpec((tm, m), lambda i: (i, 0))
    prev_args, prev_specs = _prev_specs(prev, tm)
    out_specs = [row(n) for _ in dtypes]
    out_shape = [jax.ShapeDtypeStruct((t, n), dt) for dt in dtypes]
    if prev is not None:
        out_specs, out_shape = [row(d)] + out_specs, [jax.ShapeDtypeStruct((t, d), F32)] + out_shape
    outs = pl.pallas_call(
        functools.partial(_proj_odd_body, n_act=max(len(prev_args) - 1, 0)),
        grid=(t // tm,),
        in_specs=[row(d)] + prev_specs + [_const_spec((1, d)), _layer_spec(w, layer), _const_spec((1, n))],
        out_specs=out_specs,
        out_shape=out_shape,
        compiler_params=_params(("parallel",)),
        name="proj_odd",
    )(h, *prev_args, nw, w, lb)
    return tuple(outs) if prev is not None else (h, *outs)


def _out_proj_body(*refs, n_act):
    h_ref, acts, wo_ref, (fw_ref,), o_ref, _ = _split_refs(refs, n_act, 1)
    x = _residual_rows(h_ref, acts, wo_ref)
    o_ref[...] = x * _rms_scale(x) * fw_ref[...]


def _out_proj(h, prev, tm, final_w):
    t, d = h.shape
    row = lambda n: pl.BlockSpec((tm, n), lambda i: (i, 0))
    prev_args, prev_specs = _prev_specs(prev, tm)
    return pl.pallas_call(
        functools.partial(_out_proj_body, n_act=len(prev[0])),
        grid=(t // tm,),
        in_specs=[row(d)] + prev_specs + [_const_spec((1, d))],
        out_specs=row(d),
        out_shape=jax.ShapeDtypeStruct((t, d), F32),
        compiler_params=_params(("parallel",)),
        name="out_proj",
    )(h, *prev_args, final_w)


def _softplus(x):
    return jnp.maximum(x, 0.0) + jnp.log(1.0 + jnp.exp(-jnp.abs(x)))


def _ssd_body(xbc_ref, dt_ref, z_ref, cw_ref, cb_ref, dtb_ref, a_ref, dsk_ref, nw_ref, tri_ref,
              exp_ref, y_ref, ext_ref, st_ref, yacc_ref, *, width):
    cs = SSD_CHUNK
    gw = width // SSD_GROUPS
    hpg = gw // SSD_HEAD_DIM
    pad = 8
    c = pl.program_id(1)

    @pl.when(c == 0)
    def _():
        ext_ref[0:pad, :] = jnp.zeros((pad, ext_ref.shape[1]), F32)
        st_ref[...] = jnp.zeros(st_ref.shape, F32)

    xin = xbc_ref[...]
    ext_ref[pad:pad + cs, :] = xin
    conv = cb_ref[...] + cw_ref[SSD_CONV - 1:SSD_CONV, :] * xin
    for j in range(1, SSD_CONV):
        conv = conv + cw_ref[SSD_CONV - 1 - j:SSD_CONV - j, :] * ext_ref[pad - j:pad - j + cs, :]
    ext_ref[0:pad, :] = xin[cs - pad:cs, :]
    act = _silu(conv)

    dt = _softplus(dt_ref[...] + dtb_ref[...])
    a = dt * a_ref[...]
    a_cs = _dot_exact_lhs(tri_ref[...], a, 3)
    a_cs_t = a_cs.T
    dt_t = dt.T
    a_last = a_cs[cs - 1:cs, :]
    per_head = jnp.concatenate(
        [dt * jnp.exp2(a_last - a_cs), jnp.exp2(a_cs), jnp.broadcast_to(jnp.exp2(a_last), (8, LANES))], axis=0)
    per_col = _dot_exact_rhs(per_head, exp_ref[...], 2)
    w_state = per_col[0:cs]
    e_acs = per_col[cs:2 * cs]
    e_last = per_col[2 * cs:2 * cs + 1]

    row = lax.broadcasted_iota(jnp.int32, (cs, cs), 0)
    colm = lax.broadcasted_iota(jnp.int32, (cs, cs), 1)
    tril = row >= colm
    lane = lax.broadcasted_iota(jnp.int32, (cs, LANES), 1)
    low_half = lane < SSD_HEAD_DIM

    group_cols = [slice(g * gw, (g + 1) * gw) for g in range(SSD_GROUPS)]
    cbs, y_offs = [], []
    for g, gs in enumerate(group_cols):
        bg = act[:, width + g * SSD_STATE:width + (g + 1) * SSD_STATE].astype(BF16)
        cg = act[:, width + (SSD_GROUPS + g) * SSD_STATE:width + (SSD_GROUPS + g + 1) * SSD_STATE].astype(BF16)
        cbs.append(_dot_nt(cg, bg))
        s_prev = st_ref[g]
        y_offs.append(_dot(cg, s_prev.astype(BF16)))
        xd = (act[:, gs] * w_state[:, gs]).astype(BF16)
        st_ref[g] = s_prev * e_last[:, gs] + _dot_tn(bg, xd)
    for g, gs in enumerate(group_cols):
        xg = act[:, gs]
        y_off = y_offs[g] * e_acs[:, gs]
        for p in range(gw // LANES):
            xp = xg[:, p * LANES:(p + 1) * LANES]
            halves = (jnp.where(low_half, xp, 0.0).astype(BF16), jnp.where(low_half, 0.0, xp).astype(BF16))
            acc = y_off[:, p * LANES:(p + 1) * LANES] + xp * dsk_ref[:, g * gw + p * LANES:g * gw + (p + 1) * LANES]
            for r in range(LANES // SSD_HEAD_DIM):
                hd = g * hpg + p * (LANES // SSD_HEAD_DIM) + r
                decay = jnp.exp2(jnp.where(tril, a_cs[:, hd:hd + 1] - a_cs_t[hd:hd + 1, :], NEG_INF))
                scores = (cbs[g] * decay * dt_t[hd:hd + 1, :]).astype(BF16)
                acc = acc + _dot(scores, halves[r])
            yacc_ref[:, g * gw + p * LANES:g * gw + (p + 1) * LANES] = acc

    yz = yacc_ref[...] * _silu(z_ref[...])
    y_ref[...] = (yz * _rms_scale(yz) * nw_ref[...]).astype(BF16)


def _ssd(xbc, dt, z, cw, cb, dtb, a_neg, dsk, nw, batch, seq):
    t, width = z.shape
    cs = SSD_CHUNK
    nc = seq // cs
    heads = width // SSD_HEAD_DIM
    tri = (jnp.arange(cs)[:, None] >= jnp.arange(cs)[None, :]).astype(BF16)
    expand = (jnp.arange(LANES)[:, None] == (jnp.arange(width)[None, :] // SSD_HEAD_DIM)).astype(BF16)
    pad_h = lambda v: jnp.pad(v.astype(F32), (0, LANES - heads))[None, :]
    blk = lambda n: pl.BlockSpec((cs, n), lambda b, c: (b * nc + c, 0))
    return pl.pallas_call(
        functools.partial(_ssd_body, width=width),
        grid=(batch, nc),
        in_specs=[blk(xbc.shape[1]), blk(LANES), blk(width),
                  _const_spec(cw.shape), _const_spec((1, xbc.shape[1])), _const_spec((1, LANES)),
                  _const_spec((1, LANES)), _const_spec((1, width)), _const_spec((1, width)),
                  _const_spec((cs, cs)), _const_spec((LANES, width))],
        out_specs=blk(width),
        out_shape=jax.ShapeDtypeStruct((t, width), BF16),
        scratch_shapes=[pltpu.VMEM((cs + 8, xbc.shape[1]), F32),
                        pltpu.VMEM((SSD_GROUPS, SSD_STATE, width // SSD_GROUPS), F32),
                        pltpu.VMEM((cs, width), F32)],
        compiler_params=_params(("parallel", "arbitrary")),
        name="ssd",
    )(xbc, dt, z, cw, cb[None, :], pad_h(dtb), pad_h(a_neg),
      jnp.repeat(dsk.astype(F32), SSD_HEAD_DIM)[None, :], nw[None, :], tri, expand)


def _rel_bias_tile(tab_ref, head, n_heads, delta, tk, tq):
    key = lax.broadcasted_iota(jnp.int32, (tk, tq), 0)
    qry = lax.broadcasted_iota(jnp.int32, (tk, tq), 1)
    n = jnp.maximum(delta + qry - key, 0)
    max_exact = REL_BUCKETS // 2
    large = max_exact + (jnp.log(jnp.maximum(n, 1).astype(F32) / max_exact)
                         / math.log(REL_MAX_DIST / max_exact) * (REL_BUCKETS - max_exact)).astype(jnp.int32)
    bucket = jnp.where(n < max_exact, n, jnp.minimum(large, REL_BUCKETS - 1))
    far = tab_ref[(REL_BUCKETS - 1) * n_heads + head]
    bias = jnp.zeros((tk, tq), F32)
    for j in range(REL_BUCKETS - 1):
        bias = jnp.where(bucket == j, tab_ref[j * n_heads + head] - far, bias)
    return bias * LOG2E


def _attn_body(tab_ref, lam_ref, qt_ref, k_ref, vt_ref, g_ref, sw_ref, o_ref, bias_ref, *, n_heads, seq):
    tq = tk = ATTN_BLOCK
    nq = seq // tq
    head = pl.program_id(0)

    @pl.when(pl.program_id(1) == 0)
    def _():
        for i, delta in enumerate((0, tq)):
            b = _rel_bias_tile(tab_ref, head, n_heads, delta, tk, tq)
            bias_ref[i] = jnp.concatenate([b, b], axis=1)

    lam = lam_ref[0]
    out_scale = lam_ref[1]
    feat = lax.broadcasted_iota(jnp.int32, (DA_V_DIM, tq), 0)
    first_map = feat < DA_QK_DIM
    key = lax.broadcasted_iota(jnp.int32, (tk, 2 * tq), 0)
    qry = lax.broadcasted_iota(jnp.int32, (tk, 2 * tq), 1)
    causal = jnp.where(qry >= tq, qry - tq, qry) >= key

    def scores(base, qst, k0, bias, masked):
        s = _dot(k_ref[pl.ds(base + k0, tk), :], qst)
        if bias is not None:
            s = s + bias
        if masked:
            s = jnp.where(causal, s, NEG_INF)
        return s

    def update(base, s, k0, carry):
        m, l, acc = carry
        m_new = jnp.maximum(m, jnp.max(s, axis=0, keepdims=True))
        alpha = jnp.exp2(m - m_new)
        p = jnp.exp2(s - m_new)
        v_ext = jnp.concatenate([vt_ref[:, pl.ds(base + k0, tk)], jnp.ones((ATTN_ONES_ROWS, tk), BF16)], axis=0)
        pv = _dot(v_ext, p.astype(BF16))
        l = alpha * l + pv[DA_V_DIM:DA_V_DIM + 1]
        acc = alpha * acc + pv[:DA_V_DIM]
        return m_new, l, acc

    def q_block(qi, q0):
        bases = [bi * seq for bi in range(ATTN_BATCH)]
        qsts = []
        for base in bases:
            qb = qt_ref[:, pl.ds(base + q0, tq)]
            zero = jnp.zeros_like(qb)
            qsts.append(jnp.concatenate([jnp.where(first_map, qb, zero), jnp.where(first_map, zero, qb)], axis=1))
        init = (jnp.full((1, 2 * tq), NEG_INF, F32), jnp.zeros((1, 2 * tq), F32),
                jnp.zeros((DA_V_DIM, 2 * tq), F32))

        def group(tiles, carries):
            ss = [[scores(base, qst, *tile) for base, qst in zip(bases, qsts)] for tile in tiles]
            for (k0, _, _), s_tile in zip(tiles, ss):
                carries = tuple(update(base, s, k0, cr) for base, s, cr in zip(bases, s_tile, carries))
            return carries

        far = lambda ki: (pl.multiple_of(ki * tk, tk), None, False)
        carries = (init,) * ATTN_BATCH
        if qi is None:
            carries = group([(q0, bias_ref[0], True)], carries)
        else:
            n_far = qi - 1
            carries = lax.fori_loop(
                0, n_far // 2, lambda kk, cr: group([far(2 * kk), far(2 * kk + 1)], cr), carries)
            last_two = [(pl.multiple_of(q0 - tk, tk), bias_ref[1], False), (q0, bias_ref[0], True)]
            carries = lax.cond(n_far % 2 == 1, lambda cr: group([far(n_far - 1)] + last_two, cr),
                               lambda cr: group(last_two, cr), carries)
        for base, (_, l, acc) in zip(bases, carries):
            o_t = acc[:, :tq] / l[:, :tq] - lam * (acc[:, tq:] / l[:, tq:])
            o = o_t.T
            o = o * _rms_scale(o) * sw_ref[...] * out_scale
            o_ref[pl.ds(base + q0, tq), :] = (o * _silu(g_ref[pl.ds(base + q0, tq), :])).astype(BF16)

    q_block(None, 0)

    def later_block(qi, _):
        q_block(qi, pl.multiple_of(qi * tq, tq))
        return 0

    lax.fori_loop(1, nq, later_block, 0)


def _diff_attn(qt, k, vt, g, table, lam_pair, sw, batch, seq):
    t, width = k.shape
    n_heads = width // DA_V_DIM
    blk = pl.BlockSpec((ATTN_BATCH * seq, DA_V_DIM), lambda h, b: (b, h))
    blk_t = pl.BlockSpec((DA_V_DIM, ATTN_BATCH * seq), lambda h, b: (h, b))
    smem = pl.BlockSpec(memory_space=pltpu.SMEM)
    return pl.pallas_call(
        functools.partial(_attn_body, n_heads=n_heads, seq=seq),
        grid=(n_heads, batch // ATTN_BATCH),
        in_specs=[smem, smem, blk_t, blk, blk_t, blk, _const_spec((1, DA_V_DIM))],
        out_specs=blk,
        out_shape=jax.ShapeDtypeStruct((t, width), BF16),
        scratch_shapes=[pltpu.VMEM((2, ATTN_BLOCK, 2 * ATTN_BLOCK), F32)],
        compiler_params=_params(("arbitrary", "arbitrary")),
        name="diff_attn",
    )(table.reshape(-1), lam_pair, qt, k, vt, g, sw[None, :])


def _hgrn_body(q_ref, k_ref, lf_ref, v_ref, g_ref, nw_ref, tri_ref, lvl_ref, y_ref, st_ref, b_ref):
    @pl.when(pl.program_id(2) == 0)
    def _():
        st_ref[...] = jnp.zeros(st_ref.shape, F32)

    c = q_ref.shape[0]
    dk = HG_HEAD_DIM
    heads = [slice(hi * dk, (hi + 1) * dk) for hi in range(HG_HEADS_PER_STEP)]
    lf_hi, lf_lo = _split_bf16(lf_ref[...], 2)
    b_ref[...] = _dot(tri_ref[...], jnp.concatenate([lf_hi, lf_lo], axis=0))

    outs = []
    for hi, hs in enumerate(heads):
        outs.append(_dot_nt((q_ref[:, hs] * jnp.exp2(b_ref[:, hs])).astype(BF16), st_ref[hi].astype(BF16)))
    for hi, hs in enumerate(heads):
        b_last = b_ref[c - 1:c, hs]
        kd = (k_ref[:, hs] * jnp.exp2(b_last - b_ref[:, hs])).astype(BF16)
        st_ref[hi] = st_ref[hi] * jnp.exp2(b_last) + _dot_tn(v_ref[:, hs], kd)

    row = lax.broadcasted_iota(jnp.int32, (c, dk), 0)
    lvl = lvl_ref[...]
    on_diag = lvl == 0
    atts = [jnp.where(on_diag, jnp.sum(q_ref[:, hs] * k_ref[:, hs], axis=-1, keepdims=True), 0.0)
            for hs in heads]
    m = c // 2
    while m >= 1:
        is_query = (row & m) != 0
        sign = jnp.where(is_query, 1.0, -1.0)
        zs = []
        for hs in heads:
            b = b_ref[:, hs]
            ref_rows = jnp.concatenate(
                [jnp.broadcast_to(b[u + m - 1:u + m, :], (2 * m, dk)) for u in range(0, c, 2 * m)], axis=0)
            zs.append((jnp.where(is_query, q_ref[:, hs], k_ref[:, hs])
                       * jnp.exp2((b - ref_rows) * sign)).astype(BF16))
        in_level = lvl == m
        atts = [jnp.where(in_level, _dot_nt(z, z), att) for z, att in zip(zs, atts)]
        m //= 2

    for hi, hs in enumerate(heads):
        o = outs[hi] + _dot(atts[hi].astype(BF16), v_ref[:, hs])
        o = o * _rms_scale(o) * nw_ref[...]
        y_ref[:, hs] = (o * _silu(g_ref[:, hs])).astype(BF16)


def _hgrn(q, k, lf, v, g, nw, batch, seq):
    t, width = q.shape
    c = HG_CHUNK
    nc = seq // c
    bw = HG_HEADS_PER_STEP * HG_HEAD_DIM
    idx = jnp.arange(c)
    tri = (idx[:, None] >= idx[None, :]).astype(BF16)
    tri2 = jnp.concatenate([tri, tri], axis=1)
    diff = idx[:, None] ^ idx[None, :]
    top_bit = jnp.left_shift(1, jnp.maximum(31 - lax.clz(diff), 0))
    lvl = jnp.where(idx[:, None] > idx[None, :], top_bit, jnp.where(diff == 0, 0, -1)).astype(jnp.int32)
    blk = pl.BlockSpec((c, bw), lambda b, hg, ci: (b * nc + ci, hg))
    return pl.pallas_call(
        _hgrn_body,
        grid=(batch, width // bw, nc),
        in_specs=[blk, blk, blk, blk, blk, _const_spec((1, HG_HEAD_DIM)), _const_spec((c, 2 * c)),
                  _const_spec((c, c))],
        out_specs=blk,
        out_shape=jax.ShapeDtypeStruct((t, width), BF16),
        scratch_shapes=[pltpu.VMEM((HG_HEADS_PER_STEP, HG_HEAD_DIM, HG_HEAD_DIM), F32),
                        pltpu.VMEM((c, bw), F32)],
        compiler_params=_params(("parallel", "parallel", "arbitrary")),
        name="hgrn",
    )(q, k, lf, v, g, nw[None, :], tri2, lvl)


def kernel(x, norm_w, final_norm_w, rel_bias, even_w_in, even_w_out, conv_w, conv_b, dt_bias, A_log, D_skip,
           ssd_norm_w, lambda_q1, lambda_k1, lambda_q2, lambda_k2, subln_w, odd_w_in, odd_w_out,
           hgrn_lower_bounds, hgrn_norm_w):
    batch, seq, d = x.shape
    depth = norm_w.shape[0]
    tm = PROJ_ROWS
    h = x.reshape(batch * seq, d)

    heads = A_log.shape[1]
    ssd_w = heads * SSD_HEAD_DIM
    conv_ch = conv_w.shape[2]
    da_w = rel_bias.shape[1] * DA_V_DIM
    offs = [0]
    for n in (ssd_w, conv_ch, heads, da_w, da_w, da_w, da_w):
        offs.append(offs[-1] + n)

    lb_all = jax.nn.softmax(hgrn_lower_bounds.astype(F32), axis=0)
    lb_all = jnp.cumsum(lb_all, axis=0) - lb_all[0]

    odd_w_in_bf, odd_w_out_bf, even_w_out_bf = (w.astype(BF16) for w in (odd_w_in, odd_w_out, even_w_out))
    prev = None
    for layer in range(depth):
        nw = norm_w[layer][None, :]
        if layer % 2 == 0:
            e = layer // 2
            w = even_w_in[e]
            seg = lambda i: w[:, offs[i]:offs[i + 1]]
            w_perm = jnp.concatenate(
                [seg(0), seg(1), seg(4), seg(6), jnp.pad(seg(2), ((0, 0), (0, LANES - heads)))],
                axis=1).astype(BF16)
            w_t = _transposed_bf16(jnp.concatenate([seg(3) * (DA_QK_DIM ** -0.5 * LOG2E), seg(5)], axis=1))
            h, z, xbc, k, g, dt, q_t, v_t = _proj_even(h, nw, w_perm, w_t, tm, prev)
            y_a = _ssd(xbc, dt, z, conv_w[e], conv_b[e], dt_bias[e], -jnp.exp(A_log[e].astype(F32)) * LOG2E,
                       D_skip[e], ssd_norm_w[e], batch, seq)
            lam_init = 0.8 - 0.6 * math.exp(-0.3 * layer)
            lam = (jnp.exp(jnp.sum(lambda_q1[e].astype(F32) * lambda_k1[e].astype(F32)))
                   - jnp.exp(jnp.sum(lambda_q2[e].astype(F32) * lambda_k2[e].astype(F32))) + lam_init)
            lam_pair = jnp.stack([lam, jnp.asarray(1.0 - lam_init, F32)])
            y_b = _diff_attn(q_t, k, v_t, g, rel_bias.astype(F32), lam_pair, subln_w[e], batch, seq)
            prev = ([y_a, y_b], even_w_out_bf, e)
        else:
            o = layer // 2
            h, q, k, lf, v, g = _proj_odd(h, nw, odd_w_in_bf, o, lb_all[layer][None, :], tm // 2, prev)
            y = _hgrn(q, k, lf, v, g, hgrn_norm_w[o], batch, seq)
            prev = ([y], odd_w_out_bf, o)
    h = _out_proj(h, prev, tm, final_norm_w[None, :])
    return h.reshape(batch, seq, d)
```

```python
import functools
import math

import jax
import jax.numpy as jnp
from jax import lax
from jax.experimental import pallas as pl
from jax.experimental.pallas import tpu as pltpu

F32 = jnp.float32
BF16 = jnp.bfloat16
EPS = 1e-6
NEG_INF = float("-inf")
LOG2E = math.log2(math.e)

LANES = 128
SSD_HEAD_DIM = 64
SSD_GROUPS = 4
SSD_STATE = 128
SSD_CONV = 4
SSD_CHUNK = 128
DA_QK_DIM = 64
DA_V_DIM = 128
REL_BUCKETS = 32
REL_MAX_DIST = 128
HG_HEAD_DIM = 128
HG_CHUNK = 128
HG_HEADS_PER_STEP = 16
HG_DIRECT_LEVEL = 1
ATTN_BLOCK = 256
ATTN_BATCH = 4
ATTN_ONES_ROWS = 16
PROJ_ROWS = 512
VMEM_LIMIT = 56 * 1024 * 1024


def _dot(a, b):
    return jnp.dot(a, b, preferred_element_type=F32)


def _dot_nt(a, b):
    return lax.dot_general(a, b, (((1,), (1,)), ((), ())), preferred_element_type=F32)


def _dot_tn(a, b):
    return lax.dot_general(a, b, (((0,), (0,)), ((), ())), preferred_element_type=F32)


def _split_bf16(x, parts):
    out = []
    r = x
    for _ in range(parts):
        p = r.astype(BF16)
        out.append(p)
        r = r - p.astype(F32)
    return out


def _dot_exact_rhs(x, m_bf16, parts):
    return sum(_dot(p, m_bf16) for p in _split_bf16(x, parts))


def _dot_exact_lhs(m_bf16, x, parts):
    return sum(_dot(m_bf16, p) for p in _split_bf16(x, parts))


def _sigmoid(x):
    return 1.0 / (1.0 + jnp.exp(-x))


def _silu(x):
    return x * _sigmoid(x)


def _rms_scale(x):
    return lax.rsqrt(jnp.mean(x * x, axis=-1, keepdims=True) + EPS)


def _const_spec(shape):
    nd = len(shape)
    return pl.BlockSpec(shape, lambda *_: (0,) * nd, pipeline_mode=pl.Buffered(1))


def _layer_spec(stacked, layer):
    return pl.BlockSpec((None,) + stacked.shape[1:], lambda *_: (layer, 0, 0), pipeline_mode=pl.Buffered(1))


def _params(sem):
    return pltpu.CompilerParams(dimension_semantics=sem, vmem_limit_bytes=VMEM_LIMIT)


def _residual_rows(h_ref, act_refs, wo_ref):
    x = h_ref[...]
    row = 0
    for a_ref in act_refs:
        x = x + _dot(a_ref[...], wo_ref[row:row + a_ref.shape[1], :])
        row += a_ref.shape[1]
    return x


def _block_input(h_ref, act_refs, wo_ref, hn_ref):
    if not act_refs:
        return h_ref[...]
    x = _residual_rows(h_ref, act_refs, wo_ref)
    hn_ref[...] = x
    return x


def _split_refs(refs, n_act, n_fixed):
    n_prev = n_act + 1 if n_act else 0
    acts = refs[1:1 + n_act]
    wo_ref = refs[n_act + 1] if n_act else None
    fixed = refs[1 + n_prev:1 + n_prev + n_fixed]
    outs = refs[1 + n_prev + n_fixed:]
    hn_ref = None
    if n_act:
        hn_ref, outs = outs[0], outs[1:]
    return refs[0], acts, wo_ref, fixed, hn_ref, outs


def _prev_specs(prev, tm):
    if prev is None:
        return [], []
    acts, w_out, layer = prev
    specs = [pl.BlockSpec((tm, a.shape[1]), lambda i: (i, 0)) for a in acts] + [_layer_spec(w_out, layer)]
    return list(acts) + [w_out], specs


def _proj_even_body(*refs, n_act):
    h_ref, acts, wo_ref, (nw_ref, w_ref, wt_ref), hn_ref, outs = _split_refs(refs, n_act, 3)
    z_ref, xbc_ref, k_ref, g_ref, dt_ref, qt_ref, vt_ref = outs
    x = _block_input(h_ref, acts, wo_ref, hn_ref)
    u = (x * _rms_scale(x) * nw_ref[...]).astype(BF16)
    col = 0
    for ref in (z_ref, xbc_ref, k_ref, g_ref, dt_ref):
        n = ref.shape[1]
        step = min(n, 512)
        for j in range(0, n, step):
            ref[:, j:j + step] = _dot(u, w_ref[:, col + j:col + j + step]).astype(ref.dtype)
        col += n
    row = 0
    for ref in (qt_ref, vt_ref):
        n = ref.shape[0]
        for j in range(0, n, 256):
            ref[j:j + 256, :] = _dot_nt(wt_ref[row + j:row + j + 256, :], u).astype(ref.dtype)
        row += n


def _proj_even(h, nw, w, wt, tm, prev=None):
    t, d = h.shape
    widths = (d, 2 * d, d, d, LANES)
    dtypes = (F32, F32, BF16, F32, F32)
    row = lambda n: pl.BlockSpec((tm, n), lambda i: (i, 0))
    col = pl.BlockSpec((d, tm), lambda i: (0, i))
    prev_args, prev_specs = _prev_specs(prev, tm)
    out_specs = [row(n) for n in widths] + [col, col]
    out_shape = ([jax.ShapeDtypeStruct((t, n), dt) for n, dt in zip(widths, dtypes)]
                 + [jax.ShapeDtypeStruct((d, t), BF16)] * 2)
    if prev is not None:
        out_specs, out_shape = [row(d)] + out_specs, [jax.ShapeDtypeStruct((t, d), F32)] + out_shape
    outs = pl.pallas_call(
        functools.partial(_proj_even_body, n_act=max(len(prev_args) - 1, 0)),
        grid=(t // tm,),
        in_specs=[row(d)] + prev_specs + [_const_spec((1, d)), _const_spec(w.shape), _const_spec(wt.shape)],
        out_specs=out_specs,
        out_shape=out_shape,
        compiler_params=_params(("parallel",)),
        name="proj_even",
    )(h, *prev_args, nw, w, wt)
    return tuple(outs) if prev is not None else (h, *outs)


def _proj_odd_body(*refs, n_act):
    h_ref, acts, wo_ref, (nw_ref, w_ref, lb_ref), hn_ref, outs = _split_refs(refs, n_act, 3)
    q_ref, k_ref, lf_ref, v_ref, g_ref = outs
    x = _block_input(h_ref, acts, wo_ref, hn_ref)
    u = (x * _rms_scale(x) * nw_ref[...]).astype(BF16)
    n = q_ref.shape[1]
    step = 512
    for j in range(0, n, step):
        cs = slice(j, j + step)
        q_ref[:, cs] = _silu(_dot(u, w_ref[:, j:j + step]))
        lb = lb_ref[:, cs]
        f = lb + (1.0 - lb) * _sigmoid(_dot(u, w_ref[:, n + j:n + j + step]))
        lf_ref[:, cs] = jnp.log(f) * LOG2E
        k_ref[:, cs] = 1.0 - f
        v_ref[:, cs] = _dot(u, w_ref[:, 2 * n + j:2 * n + j + step]).astype(BF16)
        g_ref[:, cs] = _dot(u, w_ref[:, 3 * n + j:3 * n + j + step])


def _proj_odd(h, nw, w, layer, lb, tm, prev=None):
    t, d = h.shape
    n = w.shape[2] // 4
    dtypes = (F32, F32, F32, BF16, F32)
    row = lambda m: pl.BlockSpec((tm, m), lambda i: (i, 0))
    prev_args, prev_specs = _prev_specs(prev, tm)
    out_specs = [row(n) for _ in dtypes]
    out_shape = [jax.ShapeDtypeStruct((t, n), dt) for dt in dtypes]
    if prev is not None:
        out_specs, out_shape = [row(d)] + out_specs, [jax.ShapeDtypeStruct((t, d), F32)] + out_shape
    outs = pl.pallas_call(
        functools.partial(_proj_odd_body, n_act=max(len(prev_args) - 1, 0)),
        grid=(t // tm,),
        in_specs=[row(d)] + prev_specs + [_const_spec((1, d)), _layer_spec(w, layer), _const_spec((1, n))],
        out_specs=out_specs,
        out_shape=out_shape,
        compiler_params=_params(("parallel",)),
        name="proj_odd",
    )(h, *prev_args, nw, w, lb)
    return tuple(outs) if prev is not None else (h, *outs)


def _out_proj_body(*refs, n_act):
    h_ref, acts, wo_ref, (fw_ref,), o_ref, _ = _split_refs(refs, n_act, 1)
    x = _residual_rows(h_ref, acts, wo_ref)
    o_ref[...] = x * _rms_scale(x) * fw_ref[...]


def _out_proj(h, prev, tm, final_w):
    t, d = h.shape
    row = lambda n: pl.BlockSpec((tm, n), lambda i: (i, 0))
    prev_args, prev_specs = _prev_specs(prev, tm)
    return pl.pallas_call(
        functools.partial(_out_proj_body, n_act=len(prev[0])),
        grid=(t // tm,),
        in_specs=[row(d)] + prev_specs + [_const_spec((1, d))],
        out_specs=row(d),
        out_shape=jax.ShapeDtypeStruct((t, d), F32),
        compiler_params=_params(("parallel",)),
        name="out_proj",
    )(h, *prev_args, final_w)


def _softplus(x):
    return jnp.maximum(x, 0.0) + jnp.log(1.0 + jnp.exp(-jnp.abs(x)))


def _ssd_body(xbc_ref, dt_ref, z_ref, cw_ref, cb_ref, dtb_ref, a_ref, dsk_ref, nw_ref, tri_ref,
              exp_ref, y_ref, ext_ref, st_ref, yacc_ref, *, width):
    cs = SSD_CHUNK
    gw = width // SSD_GROUPS
    hpg = gw // SSD_HEAD_DIM
    pad = 8
    c = pl.program_id(1)

    @pl.when(c == 0)
    def _():
        ext_ref[0:pad, :] = jnp.zeros((pad, ext_ref.shape[1]), F32)
        st_ref[...] = jnp.zeros(st_ref.shape, F32)

    xin = xbc_ref[...]
    ext_ref[pad:pad + cs, :] = xin
    conv = cb_ref[...] + cw_ref[SSD_CONV - 1:SSD_CONV, :] * xin
    for j in range(1, SSD_CONV):
        conv = conv + cw_ref[SSD_CONV - 1 - j:SSD_CONV - j, :] * ext_ref[pad - j:pad - j + cs, :]
    ext_ref[0:pad, :] = xin[cs - pad:cs, :]
    act = _silu(conv)

    dt = _softplus(dt_ref[...] + dtb_ref[...])
    a = dt * a_ref[...]
    a_cs = _dot_exact_lhs(tri_ref[...], a, 3)
    a_cs_t = a_cs.T
    dt_t = dt.T
    a_last = a_cs[cs - 1:cs, :]
    per_head = jnp.concatenate(
        [dt * jnp.exp2(a_last - a_cs), jnp.exp2(a_cs), jnp.broadcast_to(jnp.exp2(a_last), (8, LANES))], axis=0)
    per_col = _dot_exact_rhs(per_head, exp_ref[...], 2)
    w_state = per_col[0:cs]
    e_acs = per_col[cs:2 * cs]
    e_last = per_col[2 * cs:2 * cs + 1]

    row = lax.broadcasted_iota(jnp.int32, (cs, cs), 0)
    colm = lax.broadcasted_iota(jnp.int32, (cs, cs), 1)
    tril = row >= colm
    lane = lax.broadcasted_iota(jnp.int32, (cs, LANES), 1)
    low_half = lane < SSD_HEAD_DIM

    group_cols = [slice(g * gw, (g + 1) * gw) for g in range(SSD_GROUPS)]
    cbs, y_offs = [], []
    for g, gs in enumerate(group_cols):
        bg = act[:, width + g * SSD_STATE:width + (g + 1) * SSD_STATE].astype(BF16)
        cg = act[:, width + (SSD_GROUPS + g) * SSD_STATE:width + (SSD_GROUPS + g + 1) * SSD_STATE].astype(BF16)
        cbs.append(_dot_nt(cg, bg))
        s_prev = st_ref[g]
        y_offs.append(_dot(cg, s_prev.astype(BF16)))
        xd = (act[:, gs] * w_state[:, gs]).astype(BF16)
        st_ref[g] = s_prev * e_last[:, gs] + _dot_tn(bg, xd)
    for g, gs in enumerate(group_cols):
        xg = act[:, gs]
        y_off = y_offs[g] * e_acs[:, gs]
        for p in range(gw // LANES):
            xp = xg[:, p * LANES:(p + 1) * LANES]
            halves = (jnp.where(low_half, xp, 0.0).astype(BF16), jnp.where(low_half, 0.0, xp).astype(BF16))
            acc = y_off[:, p * LANES:(p + 1) * LANES] + xp * dsk_ref[:, g * gw + p * LANES:g * gw + (p + 1) * LANES]
            for r in range(LANES // SSD_HEAD_DIM):
                hd = g * hpg + p * (LANES // SSD_HEAD_DIM) + r
                decay = jnp.exp2(jnp.where(tril, a_cs[:, hd:hd + 1] - a_cs_t[hd:hd + 1, :], NEG_INF))
                scores = (cbs[g] * decay * dt_t[hd:hd + 1, :]).astype(BF16)
                acc = acc + _dot(scores, halves[r])
            yacc_ref[:, g * gw + p * LANES:g * gw + (p + 1) * LANES] = acc

    yz = yacc_ref[...] * _silu(z_ref[...])
    y_ref[...] = (yz * _rms_scale(yz) * nw_ref[...]).astype(BF16)


def _ssd(xbc, dt, z, cw, cb, dtb, a_neg, dsk, nw, batch, seq):
    t, width = z.shape
    cs = SSD_CHUNK
    nc = seq // cs
    heads = width // SSD_HEAD_DIM
    tri = (jnp.arange(cs)[:, None] >= jnp.arange(cs)[None, :]).astype(BF16)
    expand = (jnp.arange(LANES)[:, None] == (jnp.arange(width)[None, :] // SSD_HEAD_DIM)).astype(BF16)
    pad_h = lambda v: jnp.pad(v.astype(F32), (0, LANES - heads))[None, :]
    blk = lambda n: pl.BlockSpec((cs, n), lambda b, c: (b * nc + c, 0))
    return pl.pallas_call(
        functools.partial(_ssd_body, width=width),
        grid=(batch, nc),
        in_specs=[blk(xbc.shape[1]), blk(LANES), blk(width),
                  _const_spec(cw.shape), _const_spec((1, xbc.shape[1])), _const_spec((1, LANES)),
                  _const_spec((1, LANES)), _const_spec((1, width)), _const_spec((1, width)),
                  _const_spec((cs, cs)), _const_spec((LANES, width))],
        out_specs=blk(width),
        out_shape=jax.ShapeDtypeStruct((t, width), BF16),
        scratch_shapes=[pltpu.VMEM((cs + 8, xbc.shape[1]), F32),
                        pltpu.VMEM((SSD_GROUPS, SSD_STATE, width // SSD_GROUPS), F32),
                        pltpu.VMEM((cs, width), F32)],
        compiler_params=_params(("parallel", "arbitrary")),
        name="ssd",
    )(xbc, dt, z, cw, cb[None, :], pad_h(dtb), pad_h(a_neg),
      jnp.repeat(dsk.astype(F32), SSD_HEAD_DIM)[None, :], nw[None, :], tri, expand)


def _rel_bias_tile(tab_ref, head, n_heads, delta, tk, tq):
    key = lax.broadcasted_iota(jnp.int32, (tk, tq), 0)
    qry = lax.broadcasted_iota(jnp.int32, (tk, tq), 1)
    n = jnp.maximum(delta + qry - key, 0)
    max_exact = REL_BUCKETS // 2
    large = max_exact + (jnp.log(jnp.maximum(n, 1).astype(F32) / max_exact)
                         / math.log(REL_MAX_DIST / max_exact) * (REL_BUCKETS - max_exact)).astype(jnp.int32)
    bucket = jnp.where(n < max_exact, n, jnp.minimum(large, REL_BUCKETS - 1))
    far = tab_ref[(REL_BUCKETS - 1) * n_heads + head]
    bias = jnp.zeros((tk, tq), F32)
    for j in range(REL_BUCKETS - 1):
        bias = jnp.where(bucket == j, tab_ref[j * n_heads + head] - far, bias)
    return bias * LOG2E


def _attn_body(tab_ref, lam_ref, qt_ref, k_ref, vt_ref, g_ref, sw_ref, o_ref, bias_ref, *, n_heads, seq):
    tq = tk = ATTN_BLOCK
    nq = seq // tq
    head = pl.program_id(0)

    @pl.when(pl.program_id(1) == 0)
    def _():
        for i, delta in enumerate((0, tq)):
            b = _rel_bias_tile(tab_ref, head, n_heads, delta, tk, tq)
            bias_ref[i] = jnp.concatenate([b, b], axis=1)

    lam = lam_ref[0]
    out_scale = lam_ref[1]
    feat = lax.broadcasted_iota(jnp.int32, (DA_V_DIM, tq), 0)
    first_map = feat < DA_QK_DIM
    key = lax.broadcasted_iota(jnp.int32, (tk, 2 * tq), 0)
    qry = lax.broadcasted_iota(jnp.int32, (tk, 2 * tq), 1)
    causal = jnp.where(qry >= tq, qry - tq, qry) >= key

    def scores(base, qst, k0, bias, masked):
        s = _dot(k_ref[pl.ds(base + k0, tk), :], qst)
        if bias is not None:
            s = s + bias
        if masked:
            s = jnp.where(causal, s, NEG_INF)
        return s

    def update(base, s, k0, carry):
        m, l, acc = carry
        m_new = jnp.maximum(m, jnp.max(s, axis=0, keepdims=True))
        alpha = jnp.exp2(m - m_new)
        p = jnp.exp2(s - m_new)
        v_ext = jnp.concatenate([vt_ref[:, pl.ds(base + k0, tk)], jnp.ones((ATTN_ONES_ROWS, tk), BF16)], axis=0)
        pv = _dot(v_ext, p.astype(BF16))
        l = alpha * l + pv[DA_V_DIM:DA_V_DIM + 1]
        acc = alpha * acc + pv[:DA_V_DIM]
        return m_new, l, acc

    def q_block(qi, q0):
        bases = [bi * seq for bi in range(ATTN_BATCH)]
        qsts = []
        for base in bases:
            qb = qt_ref[:, pl.ds(base + q0, tq)]
            zero = jnp.zeros_like(qb)
            qsts.append(jnp.concatenate([jnp.where(first_map, qb, zero), jnp.where(first_map, zero, qb)], axis=1))
        init = (jnp.full((1, 2 * tq), NEG_INF, F32), jnp.zeros((1, 2 * tq), F32),
                jnp.zeros((DA_V_DIM, 2 * tq), F32))

        def group(tiles, carries):
            ss = [[scores(base, qst, *tile) for base, qst in zip(bases, qsts)] for tile in tiles]
            for (k0, _, _), s_tile in zip(tiles, ss):
                carries = tuple(update(base, s, k0, cr) for base, s, cr in zip(bases, s_tile, carries))
            return carries

        far = lambda ki: (pl.multiple_of(ki * tk, tk), None, False)
        carries = (init,) * ATTN_BATCH
        if qi is None:
            carries = group([(q0, bias_ref[0], True)], carries)
        else:
            n_far = qi - 1
            carries = lax.fori_loop(
                0, n_far // 2, lambda kk, cr: group([far(2 * kk), far(2 * kk + 1)], cr), carries)
            last_two = [(pl.multiple_of(q0 - tk, tk), bias_ref[1], False), (q0, bias_ref[0], True)]
            carries = lax.cond(n_far % 2 == 1, lambda cr: group([far(n_far - 1)] + last_two, cr),
                               lambda cr: group(last_two, cr), carries)
        for base, (_, l, acc) in zip(bases, carries):
            o_t = acc[:, :tq] / l[:, :tq] - lam * (acc[:, tq:] / l[:, tq:])
            o = o_t.T
            o = o * _rms_scale(o) * sw_ref[...] * out_scale
            o_ref[pl.ds(base + q0, tq), :] = (o * _silu(g_ref[pl.ds(base + q0, tq), :])).astype(BF16)

    q_block(None, 0)

    def later_block(qi, _):
        q_block(qi, pl.multiple_of(qi * tq, tq))
        return 0

    lax.fori_loop(1, nq, later_block, 0)


def _diff_attn(qt, k, vt, g, table, lam_pair, sw, batch, seq):
    t, width = k.shape
    n_heads = width // DA_V_DIM
    blk = pl.BlockSpec((ATTN_BATCH * seq, DA_V_DIM), lambda h, b: (b, h))
    blk_t = pl.BlockSpec((DA_V_DIM, ATTN_BATCH * seq), lambda h, b: (h, b))
    smem = pl.BlockSpec(memory_space=pltpu.SMEM)
    return pl.pallas_call(
        functools.partial(_attn_body, n_heads=n_heads, seq=seq),
        grid=(n_heads, batch // ATTN_BATCH),
        in_specs=[smem, smem, blk_t, blk, blk_t, blk, _const_spec((1, DA_V_DIM))],
        out_specs=blk,
        out_shape=jax.ShapeDtypeStruct((t, width), BF16),
        scratch_shapes=[pltpu.VMEM((2, ATTN_BLOCK, 2 * ATTN_BLOCK), F32)],
        compiler_params=_params(("arbitrary", "arbitrary")),
        name="diff_attn",
    )(table.reshape(-1), lam_pair, qt, k, vt, g, sw[None, :])


def _hgrn_body(q_ref, k_ref, lf_ref, v_ref, g_ref, nw_ref, tri_ref, lvl_ref, y_ref, st_ref, b_ref):
    @pl.when(pl.program_id(2) == 0)
    def _():
        st_ref[...] = jnp.zeros(st_ref.shape, F32)

    c = q_ref.shape[0]
    dk = HG_HEAD_DIM
    heads = [slice(hi * dk, (hi + 1) * dk) for hi in range(HG_HEADS_PER_STEP)]
    lf_hi, lf_lo = _split_bf16(lf_ref[...], 2)
    b_ref[...] = _dot(tri_ref[...], jnp.concatenate([lf_hi, lf_lo], axis=0))

    outs = []
    for hi, hs in enumerate(heads):
        outs.append(_dot_nt((q_ref[:, hs] * jnp.exp2(b_ref[:, hs])).astype(BF16), st_ref[hi].astype(BF16)))
    for hi, hs in enumerate(heads):
        b_last = b_ref[c - 1:c, hs]
        kd = (k_ref[:, hs] * jnp.exp2(b_last - b_ref[:, hs])).astype(BF16)
        st_ref[hi] = st_ref[hi] * jnp.exp2(b_last) + _dot_tn(v_ref[:, hs], kd)

    row = lax.broadcasted_iota(jnp.int32, (c, dk), 0)
    lvl = lvl_ref[...]
    on_diag = lvl == 0
    atts = [jnp.where(on_diag, jnp.sum(q_ref[:, hs] * k_ref[:, hs], axis=-1, keepdims=True), 0.0)
            for hs in heads]
    m = c // 2
    while m >= HG_DIRECT_LEVEL * 2:
        is_query = (row & m) != 0
        sign = jnp.where(is_query, 1.0, -1.0)
        zs = []
        for hs in heads:
            b = b_ref[:, hs]
            ref_rows = jnp.concatenate(
                [jnp.broadcast_to(b[u + m - 1:u + m, :], (2 * m, dk)) for u in range(0, c, 2 * m)], axis=0)
            zs.append((jnp.where(is_query, q_ref[:, hs], k_ref[:, hs])
                       * jnp.exp2((b - ref_rows) * sign)).astype(BF16))
        in_level = lvl == m
        atts = [jnp.where(in_level, _dot_nt(z, z), att) for z, att in zip(zs, atts)]
        m //= 2
    neighbours = lvl == HG_DIRECT_LEVEL
    for hi, hs in enumerate(heads):
        pair = jnp.sum(q_ref[:, hs] * pltpu.roll(k_ref[:, hs], 1, 0) * jnp.exp2(lf_ref[:, hs]), axis=-1, keepdims=True)
        atts[hi] = jnp.where(neighbours, pair, atts[hi])

    for hi, hs in enumerate(heads):
        o = outs[hi] + _dot(atts[hi].astype(BF16), v_ref[:, hs])
        o = o * _rms_scale(o) * nw_ref[...]
        y_ref[:, hs] = (o * _silu(g_ref[:, hs])).astype(BF16)


def _hgrn(q, k, lf, v, g, nw, batch, seq):
    t, width = q.shape
    c = HG_CHUNK
    nc = seq // c
    bw = HG_HEADS_PER_STEP * HG_HEAD_DIM
    idx = jnp.arange(c)
    tri = (idx[:, None] >= idx[None, :]).astype(BF16)
    tri2 = jnp.concatenate([tri, tri], axis=1)
    diff = idx[:, None] ^ idx[None, :]
    top_bit = jnp.left_shift(1, jnp.maximum(31 - lax.clz(diff), 0))
    lvl = jnp.where(idx[:, None] > idx[None, :], top_bit, jnp.where(diff == 0, 0, -1)).astype(jnp.int32)
    blk = pl.BlockSpec((c, bw), lambda b, hg, ci: (b * nc + ci, hg))
    return pl.pallas_call(
        _hgrn_body,
        grid=(batch, width // bw, nc),
        in_specs=[blk, blk, blk, blk, blk, _const_spec((1, HG_HEAD_DIM)), _const_spec((c, 2 * c)),
                  _const_spec((c, c))],
        out_specs=blk,
        out_shape=jax.ShapeDtypeStruct((t, width), BF16),
        scratch_shapes=[pltpu.VMEM((HG_HEADS_PER_STEP, HG_HEAD_DIM, HG_HEAD_DIM), F32),
                        pltpu.VMEM((c, bw), F32)],
        compiler_params=_params(("parallel", "parallel", "arbitrary")),
        name="hgrn",
    )(q, k, lf, v, g, nw[None, :], tri2, lvl)


def kernel(x, norm_w, final_norm_w, rel_bias, even_w_in, even_w_out, conv_w, conv_b, dt_bias, A_log, D_skip,
           ssd_norm_w, lambda_q1, lambda_k1, lambda_q2, lambda_k2, subln_w, odd_w_in, odd_w_out,
           hgrn_lower_bounds, hgrn_norm_w):
    batch, seq, d = x.shape
    depth = norm_w.shape[0]
    tm = PROJ_ROWS
    h = x.reshape(batch * seq, d)

    heads = A_log.shape[1]
    ssd_w = heads * SSD_HEAD_DIM
    conv_ch = conv_w.shape[2]
    da_w = rel_bias.shape[1] * DA_V_DIM
    offs = [0]
    for n in (ssd_w, conv_ch, heads, da_w, da_w, da_w, da_w):
        offs.append(offs[-1] + n)

    lb_all = jax.nn.softmax(hgrn_lower_bounds.astype(F32), axis=0)
    lb_all = jnp.cumsum(lb_all, axis=0) - lb_all[0]

    odd_w_in_bf, odd_w_out_bf, even_w_out_bf = (w.astype(BF16) for w in (odd_w_in, odd_w_out, even_w_out))
    prev = None
    for layer in range(depth):
        nw = norm_w[layer][None, :]
        if layer % 2 == 0:
            e = layer // 2
            w = even_w_in[e]
            seg = lambda i: w[:, offs[i]:offs[i + 1]]
            w_perm = jnp.concatenate(
                [seg(0), seg(1), seg(4), seg(6), jnp.pad(seg(2), ((0, 0), (0, LANES - heads)))],
                axis=1).astype(BF16)
            w_t = jnp.concatenate([seg(3) * (DA_QK_DIM ** -0.5 * LOG2E), seg(5)], axis=1).T.astype(BF16)
            h, z, xbc, k, g, dt, q_t, v_t = _proj_even(h, nw, w_perm, w_t, tm, prev)
            y_a = _ssd(xbc, dt, z, conv_w[e], conv_b[e], dt_bias[e], -jnp.exp(A_log[e].astype(F32)) * LOG2E,
                       D_skip[e], ssd_norm_w[e], batch, seq)
            lam_init = 0.8 - 0.6 * math.exp(-0.3 * layer)
            lam = (jnp.exp(jnp.sum(lambda_q1[e].astype(F32) * lambda_k1[e].astype(F32)))
                   - jnp.exp(jnp.sum(lambda_q2[e].astype(F32) * lambda_k2[e].astype(F32))) + lam_init)
            lam_pair = jnp.stack([lam, jnp.asarray(1.0 - lam_init, F32)])
            y_b = _diff_attn(q_t, k, v_t, g, rel_bias.astype(F32), lam_pair, subln_w[e], batch, seq)
            prev = ([y_a, y_b], even_w_out_bf, e)
        else:
            o = layer // 2
            h, q, k, lf, v, g = _proj_odd(h, nw, odd_w_in_bf, o, lb_all[layer][None, :], tm // 2, prev)
            y = _hgrn(q, k, lf, v, g, hgrn_norm_w[o], batch, seq)
            prev = ([y], odd_w_out_bf, o)
    h = _out_proj(h, prev, tm, final_norm_w[None, :])
    return h.reshape(batch, seq, d)
```

```python
import functools
import math

import jax
import jax.numpy as jnp
from jax import lax
from jax.experimental import pallas as pl
from jax.experimental.pallas import tpu as pltpu

F32 = jnp.float32
BF16 = jnp.bfloat16
EPS = 1e-6
NEG_INF = float("-inf")
LOG2E = math.log2(math.e)

LANES = 128
SSD_HEAD_DIM = 64
SSD_GROUPS = 4
SSD_STATE = 128
SSD_CONV = 4
SSD_CHUNK = 128
DA_QK_DIM = 64
DA_V_DIM = 128
REL_BUCKETS = 32
REL_MAX_DIST = 128
HG_HEAD_DIM = 128
HG_CHUNK = 128
HG_HEADS_PER_STEP = 16
HG_DIRECT_LEVEL = 1
ATTN_BLOCK = 256
ATTN_BATCH = 4
ATTN_ONES_ROWS = 16
PROJ_ROWS = 512
VMEM_LIMIT = 56 * 1024 * 1024


def _dot(a, b):
    return jnp.dot(a, b, preferred_element_type=F32)


def _dot_nt(a, b):
    return lax.dot_general(a, b, (((1,), (1,)), ((), ())), preferred_element_type=F32)


def _dot_tn(a, b):
    return lax.dot_general(a, b, (((0,), (0,)), ((), ())), preferred_element_type=F32)


def _split_bf16(x, parts):
    out = []
    r = x
    for _ in range(parts):
        p = r.astype(BF16)
        out.append(p)
        r = r - p.astype(F32)
    return out


def _dot_exact_rhs(x, m_bf16, parts):
    return sum(_dot(p, m_bf16) for p in _split_bf16(x, parts))


def _dot_exact_lhs(m_bf16, x, parts):
    return sum(_dot(m_bf16, p) for p in _split_bf16(x, parts))


def _sigmoid(x):
    return 1.0 / (1.0 + jnp.exp(-x))


def _silu(x):
    return x * _sigmoid(x)


def _rms_scale(x):
    return lax.rsqrt(jnp.mean(x * x, axis=-1, keepdims=True) + EPS)


def _const_spec(shape):
    nd = len(shape)
    return pl.BlockSpec(shape, lambda *_: (0,) * nd, pipeline_mode=pl.Buffered(1))


def _layer_spec(stacked, layer):
    return pl.BlockSpec((None,) + stacked.shape[1:], lambda *_: (layer, 0, 0), pipeline_mode=pl.Buffered(1))


def _params(sem):
    return pltpu.CompilerParams(dimension_semantics=sem, vmem_limit_bytes=VMEM_LIMIT)


def _residual_rows(h_ref, act_refs, wo_ref):
    x = h_ref[...]
    row = 0
    for a_ref in act_refs:
        x = x + _dot(a_ref[...], wo_ref[row:row + a_ref.shape[1], :])
        row += a_ref.shape[1]
    return x


def _block_input(h_ref, act_refs, wo_ref, hn_ref):
    if not act_refs:
        return h_ref[...]
    x = _residual_rows(h_ref, act_refs, wo_ref)
    hn_ref[...] = x
    return x


def _split_refs(refs, n_act, n_fixed):
    n_prev = n_act + 1 if n_act else 0
    acts = refs[1:1 + n_act]
    wo_ref = refs[n_act + 1] if n_act else None
    fixed = refs[1 + n_prev:1 + n_prev + n_fixed]
    outs = refs[1 + n_prev + n_fixed:]
    hn_ref = None
    if n_act:
        hn_ref, outs = outs[0], outs[1:]
    return refs[0], acts, wo_ref, fixed, hn_ref, outs


def _prev_specs(prev, tm):
    if prev is None:
        return [], []
    acts, w_out, layer = prev
    specs = [pl.BlockSpec((tm, a.shape[1]), lambda i: (i, 0)) for a in acts] + [_layer_spec(w_out, layer)]
    return list(acts) + [w_out], specs


def _proj_even_body(*refs, n_act):
    h_ref, acts, wo_ref, (nw_ref, w_ref, wt_ref), hn_ref, outs = _split_refs(refs, n_act, 3)
    z_ref, xbc_ref, k_ref, g_ref, dt_ref, qt_ref, vt_ref = outs
    x = _block_input(h_ref, acts, wo_ref, hn_ref)
    u = (x * _rms_scale(x) * nw_ref[...]).astype(BF16)
    col = 0
    for ref in (z_ref, xbc_ref, k_ref, g_ref, dt_ref):
        n = ref.shape[1]
        step = min(n, 512)
        for j in range(0, n, step):
            ref[:, j:j + step] = _dot(u, w_ref[:, col + j:col + j + step]).astype(ref.dtype)
        col += n
    row = 0
    for ref in (qt_ref, vt_ref):
        n = ref.shape[0]
        for j in range(0, n, 256):
            ref[j:j + 256, :] = _dot_nt(wt_ref[row + j:row + j + 256, :], u).astype(ref.dtype)
        row += n


def _proj_even(h, nw, w, wt, tm, prev=None):
    t, d = h.shape
    widths = (d, 2 * d, d, d, LANES)
    dtypes = (F32, F32, BF16, F32, F32)
    row = lambda n: pl.BlockSpec((tm, n), lambda i: (i, 0))
    col = pl.BlockSpec((d, tm), lambda i: (0, i))
    prev_args, prev_specs = _prev_specs(prev, tm)
    out_specs = [row(n) for n in widths] + [col, col]
    out_shape = ([jax.ShapeDtypeStruct((t, n), dt) for n, dt in zip(widths, dtypes)]
                 + [jax.ShapeDtypeStruct((d, t), BF16)] * 2)
    if prev is not None:
        out_specs, out_shape = [row(d)] + out_specs, [jax.ShapeDtypeStruct((t, d), F32)] + out_shape
    outs = pl.pallas_call(
        functools.partial(_proj_even_body, n_act=max(len(prev_args) - 1, 0)),
        grid=(t // tm,),
        in_specs=[row(d)] + prev_specs + [_const_spec((1, d)), _const_spec(w.shape), _const_spec(wt.shape)],
        out_specs=out_specs,
        out_shape=out_shape,
        compiler_params=_params(("parallel",)),
        name="proj_even",
    )(h, *prev_args, nw, w, wt)
    return tuple(outs) if prev is not None else (h, *outs)


def _proj_odd_body(*refs, n_act):
    h_ref, acts, wo_ref, (nw_ref, w_ref, lb_ref), hn_ref, outs = _split_refs(refs, n_act, 3)
    q_ref, k_ref, lf_ref, v_ref, g_ref = outs
    x = _block_input(h_ref, acts, wo_ref, hn_ref)
    u = (x * _rms_scale(x) * nw_ref[...]).astype(BF16)
    n = q_ref.shape[1]
    step = 512
    for j in range(0, n, step):
        cs = slice(j, j + step)
        q_ref[:, cs] = _silu(_dot(u, w_ref[:, j:j + step]))
        lb = lb_ref[:, cs]
        f = lb + (1.0 - lb) * _sigmoid(_dot(u, w_ref[:, n + j:n + j + step]))
        lf_ref[:, cs] = jnp.log(f) * LOG2E
        k_ref[:, cs] = 1.0 - f
        v_ref[:, cs] = _dot(u, w_ref[:, 2 * n + j:2 * n + j + step]).astype(BF16)
        g_ref[:, cs] = _dot(u, w_ref[:, 3 * n + j:3 * n + j + step])


def _proj_odd(h, nw, w, layer, lb, tm, prev=None):
    t, d = h.shape
    n = w.shape[2] // 4
    dtypes = (F32, F32, F32, BF16, F32)
    row = lambda m: pl.BlockSpec((tm, m), lambda i: (i, 0))
    prev_args, prev_specs = _prev_specs(prev, tm)
    out_specs = [row(n) for _ in dtypes]
    out_shape = [jax.ShapeDtypeStruct((t, n), dt) for dt in dtypes]
    if prev is not None:
        out_specs, out_shape = [row(d)] + out_specs, [jax.ShapeDtypeStruct((t, d), F32)] + out_shape
    outs = pl.pallas_call(
        functools.partial(_proj_odd_body, n_act=max(len(prev_args) - 1, 0)),
        grid=(t // tm,),
        in_specs=[row(d)] + prev_specs + [_const_spec((1, d)), _layer_spec(w, layer), _const_spec((1, n))],
        out_specs=out_specs,
        out_shape=out_shape,
        compiler_params=_params(("parallel",)),
        name="proj_odd",
    )(h, *prev_args, nw, w, lb)
    return tuple(outs) if prev is not None else (h, *outs)


def _out_proj_body(*refs, n_act):
    h_ref, acts, wo_ref, (fw_ref,), o_ref, _ = _split_refs(refs, n_act, 1)
    x = _residual_rows(h_ref, acts, wo_ref)
    o_ref[...] = x * _rms_scale(x) * fw_ref[...]


def _out_proj(h, prev, tm, final_w):
    t, d = h.shape
    row = lambda n: pl.BlockSpec((tm, n), lambda i: (i, 0))
    prev_args, prev_specs = _prev_specs(prev, tm)
    return pl.pallas_call(
        functools.partial(_out_proj_body, n_act=len(prev[0])),
        grid=(t // tm,),
        in_specs=[row(d)] + prev_specs + [_const_spec((1, d))],
        out_specs=row(d),
        out_shape=jax.ShapeDtypeStruct((t, d), F32),
        compiler_params=_params(("parallel",)),
        name="out_proj",
    )(h, *prev_args, final_w)


def _softplus(x):
    return jnp.maximum(x, 0.0) + jnp.log(1.0 + jnp.exp(-jnp.abs(x)))


def _ssd_body(xbc_ref, dt_ref, z_ref, cw_ref, cb_ref, dtb_ref, a_ref, dsk_ref, nw_ref, tri_ref,
              exp_ref, y_ref, ext_ref, st_ref, yacc_ref, *, width):
    cs = SSD_CHUNK
    gw = width // SSD_GROUPS
    hpg = gw // SSD_HEAD_DIM
    pad = 8
    c = pl.program_id(1)

    @pl.when(c == 0)
    def _():
        ext_ref[0:pad, :] = jnp.zeros((pad, ext_ref.shape[1]), F32)
        st_ref[...] = jnp.zeros(st_ref.shape, F32)

    xin = xbc_ref[...]
    ext_ref[pad:pad + cs, :] = xin
    conv = cb_ref[...] + cw_ref[SSD_CONV - 1:SSD_CONV, :] * xin
    for j in range(1, SSD_CONV):
        conv = conv + cw_ref[SSD_CONV - 1 - j:SSD_CONV - j, :] * ext_ref[pad - j:pad - j + cs, :]
    ext_ref[0:pad, :] = xin[cs - pad:cs, :]
    act = _silu(conv)

    dt = _softplus(dt_ref[...] + dtb_ref[...])
    a = dt * a_ref[...]
    a_cs = _dot_exact_lhs(tri_ref[...], a, 3)
    a_cs_t = a_cs.T
    dt_t = dt.T
    a_last = a_cs[cs - 1:cs, :]
    per_head = jnp.concatenate(
        [dt * jnp.exp2(a_last - a_cs), jnp.exp2(a_cs), jnp.broadcast_to(jnp.exp2(a_last), (8, LANES))], axis=0)
    per_col = _dot_exact_rhs(per_head, exp_ref[...], 2)
    w_state = per_col[0:cs]
    e_acs = per_col[cs:2 * cs]
    e_last = per_col[2 * cs:2 * cs + 1]

    row = lax.broadcasted_iota(jnp.int32, (cs, cs), 0)
    colm = lax.broadcasted_iota(jnp.int32, (cs, cs), 1)
    tril = row >= colm
    lane = lax.broadcasted_iota(jnp.int32, (cs, LANES), 1)
    low_half = lane < SSD_HEAD_DIM

    group_cols = [slice(g * gw, (g + 1) * gw) for g in range(SSD_GROUPS)]
    cbs, y_offs = [], []
    for g, gs in enumerate(group_cols):
        bg = act[:, width + g * SSD_STATE:width + (g + 1) * SSD_STATE].astype(BF16)
        cg = act[:, width + (SSD_GROUPS + g) * SSD_STATE:width + (SSD_GROUPS + g + 1) * SSD_STATE].astype(BF16)
        cbs.append(_dot_nt(cg, bg))
        s_prev = st_ref[g]
        y_offs.append(_dot(cg, s_prev.astype(BF16)))
        xd = (act[:, gs] * w_state[:, gs]).astype(BF16)
        st_ref[g] = s_prev * e_last[:, gs] + _dot_tn(bg, xd)
    for g, gs in enumerate(group_cols):
        xg = act[:, gs]
        y_off = y_offs[g] * e_acs[:, gs]
        for p in range(gw // LANES):
            xp = xg[:, p * LANES:(p + 1) * LANES]
            halves = (jnp.where(low_half, xp, 0.0).astype(BF16), jnp.where(low_half, 0.0, xp).astype(BF16))
            acc = y_off[:, p * LANES:(p + 1) * LANES] + xp * dsk_ref[:, g * gw + p * LANES:g * gw + (p + 1) * LANES]
            for r in range(LANES // SSD_HEAD_DIM):
                hd = g * hpg + p * (LANES // SSD_HEAD_DIM) + r
                decay = jnp.exp2(jnp.where(tril, a_cs[:, hd:hd + 1] - a_cs_t[hd:hd + 1, :], NEG_INF))
                scores = (cbs[g] * decay * dt_t[hd:hd + 1, :]).astype(BF16)
                acc = acc + _dot(scores, halves[r])
            yacc_ref[:, g * gw + p * LANES:g * gw + (p + 1) * LANES] = acc

    yz = yacc_ref[...] * _silu(z_ref[...])
    y_ref[...] = (yz * _rms_scale(yz) * nw_ref[...]).astype(BF16)


def _ssd(xbc, dt, z, cw, cb, dtb, a_neg, dsk, nw, batch, seq):
    t, width = z.shape
    cs = SSD_CHUNK
    nc = seq // cs
    heads = width // SSD_HEAD_DIM
    tri = (jnp.arange(cs)[:, None] >= jnp.arange(cs)[None, :]).astype(BF16)
    expand = (jnp.arange(LANES)[:, None] == (jnp.arange(width)[None, :] // SSD_HEAD_DIM)).astype(BF16)
    pad_h = lambda v: jnp.pad(v.astype(F32), (0, LANES - heads))[None, :]
    blk = lambda n: pl.BlockSpec((cs, n), lambda b, c: (b * nc + c, 0))
    return pl.pallas_call(
        functools.partial(_ssd_body, width=width),
        grid=(batch, nc),
        in_specs=[blk(xbc.shape[1]), blk(LANES), blk(width),
                  _const_spec(cw.shape), _const_spec((1, xbc.shape[1])), _const_spec((1, LANES)),
                  _const_spec((1, LANES)), _const_spec((1, width)), _const_spec((1, width)),
                  _const_spec((cs, cs)), _const_spec((LANES, width))],
        out_specs=blk(width),
        out_shape=jax.ShapeDtypeStruct((t, width), BF16),
        scratch_shapes=[pltpu.VMEM((cs + 8, xbc.shape[1]), F32),
                        pltpu.VMEM((SSD_GROUPS, SSD_STATE, width // SSD_GROUPS), F32),
                        pltpu.VMEM((cs, width), F32)],
        compiler_params=_params(("parallel", "arbitrary")),
        name="ssd",
    )(xbc, dt, z, cw, cb[None, :], pad_h(dtb), pad_h(a_neg),
      jnp.repeat(dsk.astype(F32), SSD_HEAD_DIM)[None, :], nw[None, :], tri, expand)


def _rel_bias_tile(tab_ref, head, n_heads, delta, tk, tq):
    key = lax.broadcasted_iota(jnp.int32, (tk, tq), 0)
    qry = lax.broadcasted_iota(jnp.int32, (tk, tq), 1)
    n = jnp.maximum(delta + qry - key, 0)
    max_exact = REL_BUCKETS // 2
    large = max_exact + (jnp.log(jnp.maximum(n, 1).astype(F32) / max_exact)
                         / math.log(REL_MAX_DIST / max_exact) * (REL_BUCKETS - max_exact)).astype(jnp.int32)
    bucket = jnp.where(n < max_exact, n, jnp.minimum(large, REL_BUCKETS - 1))
    far = tab_ref[(REL_BUCKETS - 1) * n_heads + head]
    bias = jnp.zeros((tk, tq), F32)
    for j in range(REL_BUCKETS - 1):
        bias = jnp.where(bucket == j, tab_ref[j * n_heads + head] - far, bias)
    return bias * LOG2E


def _attn_body(tab_ref, lam_ref, qt_ref, k_ref, vt_ref, g_ref, sw_ref, o_ref, bias_ref, *, n_heads, seq):
    tq = tk = ATTN_BLOCK
    nq = seq // tq
    head = pl.program_id(0)

    @pl.when(pl.program_id(1) == 0)
    def _():
        for i, delta in enumerate((0, tq)):
            b = _rel_bias_tile(tab_ref, head, n_heads, delta, tk, tq)
            bias_ref[i] = jnp.concatenate([b, b], axis=1)

    lam = lam_ref[0]
    out_scale = lam_ref[1]
    feat = lax.broadcasted_iota(jnp.int32, (DA_V_DIM, tq), 0)
    first_map = feat < DA_QK_DIM
    key = lax.broadcasted_iota(jnp.int32, (tk, 2 * tq), 0)
    qry = lax.broadcasted_iota(jnp.int32, (tk, 2 * tq), 1)
    causal = jnp.where(qry >= tq, qry - tq, qry) >= key

    def scores(base, qst, k0, bias, masked):
        s = _dot(k_ref[pl.ds(base + k0, tk), :], qst)
        if bias is not None:
            s = s + bias
        if masked:
            s = jnp.where(causal, s, NEG_INF)
        return s

    def update(base, s, k0, carry):
        m, l, acc = carry
        m_new = jnp.maximum(m, jnp.max(s, axis=0, keepdims=True))
        alpha = jnp.exp2(m - m_new)
        p = jnp.exp2(s - m_new)
        v_ext = jnp.concatenate([vt_ref[:, pl.ds(base + k0, tk)], jnp.ones((ATTN_ONES_ROWS, tk), BF16)], axis=0)
        pv = _dot(v_ext, p.astype(BF16))
        l = alpha * l + pv[DA_V_DIM:DA_V_DIM + 1]
        acc = alpha * acc + pv[:DA_V_DIM]
        return m_new, l, acc

    def q_block(qi, q0):
        bases = [bi * seq for bi in range(ATTN_BATCH)]
        qsts = []
        for base in bases:
            qb = qt_ref[:, pl.ds(base + q0, tq)]
            zero = jnp.zeros_like(qb)
            qsts.append(jnp.concatenate([jnp.where(first_map, qb, zero), jnp.where(first_map, zero, qb)], axis=1))
        init = (jnp.full((1, 2 * tq), NEG_INF, F32), jnp.zeros((1, 2 * tq), F32),
                jnp.zeros((DA_V_DIM, 2 * tq), F32))

        def group(tiles, carries):
            ss = [[scores(base, qst, *tile) for base, qst in zip(bases, qsts)] for tile in tiles]
            for (k0, _, _), s_tile in zip(tiles, ss):
                carries = tuple(update(base, s, k0, cr) for base, s, cr in zip(bases, s_tile, carries))
            return carries

        far = lambda ki: (pl.multiple_of(ki * tk, tk), None, False)
        carries = (init,) * ATTN_BATCH
        if qi is None:
            carries = group([(q0, bias_ref[0], True)], carries)
        else:
            n_far = qi - 1
            carries = lax.fori_loop(
                0, n_far // 2, lambda kk, cr: group([far(2 * kk), far(2 * kk + 1)], cr), carries)
            last_two = [(pl.multiple_of(q0 - tk, tk), bias_ref[1], False), (q0, bias_ref[0], True)]
            carries = lax.cond(n_far % 2 == 1, lambda cr: group([far(n_far - 1)] + last_two, cr),
                               lambda cr: group(last_two, cr), carries)
        for base, (_, l, acc) in zip(bases, carries):
            o_t = acc[:, :tq] / l[:, :tq] - lam * (acc[:, tq:] / l[:, tq:])
            o = o_t.T
            o = o * _rms_scale(o) * sw_ref[...] * out_scale
            o_ref[pl.ds(base + q0, tq), :] = (o * _silu(g_ref[pl.ds(base + q0, tq), :])).astype(BF16)

    q_block(None, 0)

    def later_block(qi, _):
        q_block(qi, pl.multiple_of(qi * tq, tq))
        return 0

    lax.fori_loop(1, nq, later_block, 0)


def _diff_attn(qt, k, vt, g, table, lam_pair, sw, batch, seq):
    t, width = k.shape
    n_heads = width // DA_V_DIM
    blk = pl.BlockSpec((ATTN_BATCH * seq, DA_V_DIM), lambda h, b: (b, h))
    blk_t = pl.BlockSpec((DA_V_DIM, ATTN_BATCH * seq), lambda h, b: (h, b))
    smem = pl.BlockSpec(memory_space=pltpu.SMEM)
    return pl.pallas_call(
        functools.partial(_attn_body, n_heads=n_heads, seq=seq),
        grid=(n_heads, batch // ATTN_BATCH),
        in_specs=[smem, smem, blk_t, blk, blk_t, blk, _const_spec((1, DA_V_DIM))],
        out_specs=blk,
        out_shape=jax.ShapeDtypeStruct((t, width), BF16),
        scratch_shapes=[pltpu.VMEM((2, ATTN_BLOCK, 2 * ATTN_BLOCK), F32)],
        compiler_params=_params(("arbitrary", "arbitrary")),
        name="diff_attn",
    )(table.reshape(-1), lam_pair, qt, k, vt, g, sw[None, :])


def _hgrn_body(q_ref, k_ref, lf_ref, v_ref, g_ref, nw_ref, tri_ref, lvl_ref, y_ref, st_ref, b_ref):
    @pl.when(pl.program_id(2) == 0)
    def _():
        st_ref[...] = jnp.zeros(st_ref.shape, F32)

    c = q_ref.shape[0]
    dk = HG_HEAD_DIM
    heads = [slice(hi * dk, (hi + 1) * dk) for hi in range(HG_HEADS_PER_STEP)]
    lf_hi, lf_lo = _split_bf16(lf_ref[...], 2)
    b_ref[...] = _dot(tri_ref[...], jnp.concatenate([lf_hi, lf_lo], axis=0))

    outs = []
    for hi, hs in enumerate(heads):
        outs.append(_dot_nt((q_ref[:, hs] * jnp.exp2(b_ref[:, hs])).astype(BF16), st_ref[hi].astype(BF16)))
    for hi, hs in enumerate(heads):
        b_last = b_ref[c - 1:c, hs]
        kd = (k_ref[:, hs] * jnp.exp2(b_last - b_ref[:, hs])).astype(BF16)
        st_ref[hi] = st_ref[hi] * jnp.exp2(b_last) + _dot_tn(v_ref[:, hs], kd)

    row = lax.broadcasted_iota(jnp.int32, (c, dk), 0)
    lvl = lvl_ref[...]
    on_diag = lvl == 0
    atts = [jnp.where(on_diag, jnp.sum(q_ref[:, hs] * k_ref[:, hs], axis=-1, keepdims=True), 0.0)
            for hs in heads]
    m = c // 2
    while m >= HG_DIRECT_LEVEL * 2:
        is_query = (row & m) != 0
        sign = jnp.where(is_query, 1.0, -1.0)
        zs = []
        for hs in heads:
            b = b_ref[:, hs]
            ref_rows = jnp.concatenate(
                [jnp.broadcast_to(b[u + m - 1:u + m, :], (2 * m, dk)) for u in range(0, c, 2 * m)], axis=0)
            zs.append((jnp.where(is_query, q_ref[:, hs], k_ref[:, hs])
                       * jnp.exp2((b - ref_rows) * sign)).astype(BF16))
        in_level = lvl == m
        atts = [jnp.where(in_level, _dot_nt(z, z), att) for z, att in zip(zs, atts)]
        m //= 2
    neighbours = lvl == HG_DIRECT_LEVEL
    for hi, hs in enumerate(heads):
        pair = jnp.sum(q_ref[:, hs] * pltpu.roll(k_ref[:, hs], 1, 0) * jnp.exp2(lf_ref[:, hs]), axis=-1, keepdims=True)
        atts[hi] = jnp.where(neighbours, pair, atts[hi])

    for hi, hs in enumerate(heads):
        o = outs[hi] + _dot(atts[hi].astype(BF16), v_ref[:, hs])
        o = o * _rms_scale(o) * nw_ref[...]
        y_ref[:, hs] = (o * _silu(g_ref[:, hs])).astype(BF16)


def _hgrn(q, k, lf, v, g, nw, batch, seq):
    t, width = q.shape
    c = HG_CHUNK
    nc = seq // c
    bw = HG_HEADS_PER_STEP * HG_HEAD_DIM
    idx = jnp.arange(c)
    tri = (idx[:, None] >= idx[None, :]).astype(BF16)
    tri2 = jnp.concatenate([tri, tri], axis=1)
    diff = idx[:, None] ^ idx[None, :]
    top_bit = jnp.left_shift(1, jnp.maximum(31 - lax.clz(diff), 0))
    lvl = jnp.where(idx[:, None] > idx[None, :], top_bit, jnp.where(diff == 0, 0, -1)).astype(jnp.int32)
    blk = pl.BlockSpec((c, bw), lambda b, hg, ci: (b * nc + ci, hg))
    return pl.pallas_call(
        _hgrn_body,
        grid=(batch, width // bw, nc),
        in_specs=[blk, blk, blk, blk, blk, _const_spec((1, HG_HEAD_DIM)), _const_spec((c, 2 * c)),
                  _const_spec((c, c))],
        out_specs=blk,
        out_shape=jax.ShapeDtypeStruct((t, width), BF16),
        scratch_shapes=[pltpu.VMEM((HG_HEADS_PER_STEP, HG_HEAD_DIM, HG_HEAD_DIM), F32),
                        pltpu.VMEM((c, bw), F32)],
        compiler_params=_params(("parallel", "parallel", "arbitrary")),
        name="hgrn",
    )(q, k, lf, v, g, nw[None, :], tri2, lvl)


def kernel(x, norm_w, final_norm_w, rel_bias, even_w_in, even_w_out, conv_w, conv_b, dt_bias, A_log, D_skip,
           ssd_norm_w, lambda_q1, lambda_k1, lambda_q2, lambda_k2, subln_w, odd_w_in, odd_w_out,
           hgrn_lower_bounds, hgrn_norm_w):
    batch, seq, d = x.shape
    depth = norm_w.shape[0]
    tm = PROJ_ROWS
    assert batch % ATTN_BATCH == 0 and seq % ATTN_BLOCK == 0 and seq % tm == 0
    assert seq % SSD_CHUNK == 0 and seq % HG_CHUNK == 0
    assert (odd_w_in.shape[2] // 4) % (HG_HEADS_PER_STEP * HG_HEAD_DIM) == 0
    h = x.reshape(batch * seq, d)

    heads = A_log.shape[1]
    ssd_w = heads * SSD_HEAD_DIM
    conv_ch = conv_w.shape[2]
    da_w = rel_bias.shape[1] * DA_V_DIM
    offs = [0]
    for n in (ssd_w, conv_ch, heads, da_w, da_w, da_w, da_w):
        offs.append(offs[-1] + n)

    lb_all = jax.nn.softmax(hgrn_lower_bounds.astype(F32), axis=0)
    lb_all = jnp.cumsum(lb_all, axis=0) - lb_all[0]

    odd_w_in_bf, odd_w_out_bf, even_w_out_bf = (w.astype(BF16) for w in (odd_w_in, odd_w_out, even_w_out))
    prev = None
    for layer in range(depth):
        nw = norm_w[layer][None, :]
        if layer % 2 == 0:
            e = layer // 2
            w = even_w_in[e]
            seg = lambda i: w[:, offs[i]:offs[i + 1]]
            w_perm = jnp.concatenate(
                [seg(0), seg(1), seg(4), seg(6), jnp.pad(seg(2), ((0, 0), (0, LANES - heads)))],
                axis=1).astype(BF16)
            w_t = jnp.concatenate([seg(3) * (DA_QK_DIM ** -0.5 * LOG2E), seg(5)], axis=1).T.astype(BF16)
            h, z, xbc, k, g, dt, q_t, v_t = _proj_even(h, nw, w_perm, w_t, tm, prev)
            y_a = _ssd(xbc, dt, z, conv_w[e], conv_b[e], dt_bias[e], -jnp.exp(A_log[e].astype(F32)) * LOG2E,
                       D_skip[e], ssd_norm_w[e], batch, seq)
            lam_init = 0.8 - 0.6 * math.exp(-0.3 * layer)
            lam = (jnp.exp(jnp.sum(lambda_q1[e].astype(F32) * lambda_k1[e].astype(F32)))
                   - jnp.exp(jnp.sum(lambda_q2[e].astype(F32) * lambda_k2[e].astype(F32))) + lam_init)
            lam_pair = jnp.stack([lam, jnp.asarray(1.0 - lam_init, F32)])
            y_b = _diff_attn(q_t, k, v_t, g, rel_bias.astype(F32), lam_pair, subln_w[e], batch, seq)
            prev = ([y_a, y_b], even_w_out_bf, e)
        else:
            o = layer // 2
            h, q, k, lf, v, g = _proj_odd(h, nw, odd_w_in_bf, o, lb_all[layer][None, :], tm // 2, prev)
            y = _hgrn(q, k, lf, v, g, hgrn_norm_w[o], batch, seq)
            prev = ([y], odd_w_out_bf, o)
    h = _out_proj(h, prev, tm, final_norm_w[None, :])
    return h.reshape(batch, seq, d)
```
